```python
import math
import jax, jax.numpy as jnp
from jax import lax
import numpy as np

D_MODEL = 2048
BATCH = 4
SEQ = 8192
DEPTH = 4
DEC_BATCH = 1
DEC_SEQ = 16384
PAST_LEN = 128

GRID_W = 64
D_CONV = D_MODEL // 2
CONV_K = 31
CONV_PAD = CONV_K // 2
HEAD_DIM = 128
N_HEADS = D_MODEL // HEAD_DIM
N_KV = 4
GROUP = N_HEADS // N_KV
D_ATTN = N_HEADS * HEAD_DIM
D_KV = N_KV * HEAD_DIM
ROPE_AXIS_DIM = HEAD_DIM // 2
ROPE_FREQS = ROPE_AXIS_DIM // 2
ROPE_THETA = 10000.0
Q_BLOCK = 128
EPS = 1e-6
IN_SPLITS = (D_CONV, D_CONV, D_CONV, D_ATTN, D_KV, D_KV, D_ATTN, D_MODEL, D_MODEL)
N_IN = sum(IN_SPLITS)

kernel_name = "hybrid_conformer_gqa_axial_rope_encoder"


def _rmsnorm(x, g):
    xf = x.astype(jnp.float32)
    y = xf * lax.rsqrt(jnp.mean(xf * xf, axis=-1, keepdims=True) + EPS)
    return (y * g.astype(jnp.float32)).astype(x.dtype)


def _layernorm(x, g, b):
    xf = x.astype(jnp.float32)
    mu = jnp.mean(xf, axis=-1, keepdims=True)
    xc = xf - mu
    var = jnp.mean(xc * xc, axis=-1, keepdims=True)
    y = xc * lax.rsqrt(var + EPS) * g.astype(jnp.float32) + b.astype(jnp.float32)
    return y.astype(x.dtype)


def _axial_rope_tables(seq_len):
    rows = seq_len // GRID_W
    row_idx = jnp.broadcast_to(jnp.arange(rows, dtype=jnp.float32)[:, None], (rows, GRID_W)).reshape(-1)
    col_idx = jnp.broadcast_to(jnp.arange(GRID_W, dtype=jnp.float32)[None, :], (rows, GRID_W)).reshape(-1)
    inv_freq = ROPE_THETA ** (-(jnp.arange(ROPE_FREQS, dtype=jnp.float32) * 2.0) / ROPE_AXIS_DIM)
    ang = jnp.stack([row_idx[:, None] * inv_freq, col_idx[:, None] * inv_freq], axis=1)
    return jnp.cos(ang), jnp.sin(ang)


def _apply_axial_rope(x, cos, sin):
    shp = x.shape
    xs = x.reshape(shp[:-1] + (2, 2, ROPE_FREQS))
    xa = xs[..., 0, :]
    xb = xs[..., 1, :]
    c = cos[None, :, None]
    s = sin[None, :, None]
    out = jnp.stack([xa * c - xb * s, xb * c + xa * s], axis=-2)
    return out.reshape(shp)


def _block_attention(q, k, v):
    b, s = q.shape[0], q.shape[1]
    nb = s // Q_BLOCK
    scale = 1.0 / math.sqrt(HEAD_DIM)
    qb = q.reshape(b, nb, Q_BLOCK, N_KV, GROUP, HEAD_DIM).transpose(1, 0, 2, 3, 4, 5)

    def one_block(qi):
        sc = jnp.einsum('bqkgd,bskd->bkgqs', qi, k, preferred_element_type=jnp.float32) * scale
        p = jax.nn.softmax(sc, axis=-1)
        return jnp.einsum('bkgqs,bskd->bqkgd', p.astype(v.dtype), v)

    o = lax.map(one_block, qb)
    return o.transpose(1, 0, 2, 3, 4, 5).reshape(b, s, D_ATTN)


def _layer(x, pre_g, w_in, dw_k, dw_b, ln_g, ln_b, w_conv_out, qn_g, kn_g, w_attn_o, w_out, post_g):
    b, s, _ = x.shape
    h = _rmsnorm(x, pre_g)
    proj = jnp.einsum('bsd,dn->bsn', h, w_in)
    idx = list(np.cumsum(IN_SPLITS)[:-1])
    glu_a, glu_b, z_conv, q, k, v, z_attn, g_conv, g_attn = jnp.split(proj, idx, axis=-1)

    u = glu_a * jax.nn.sigmoid(glu_b)
    u = lax.conv_general_dilated(
        u, dw_k.astype(u.dtype), window_strides=(1,), padding=[(CONV_PAD, CONV_PAD)],
        dimension_numbers=('NWC', 'WIO', 'NWC'), feature_group_count=D_CONV) + dw_b
    u = jax.nn.silu(_layernorm(u, ln_g, ln_b))
    u = u * jax.nn.silu(z_conv)
    y_conv = jnp.einsum('bsc,cd->bsd', u, w_conv_out)

    cos, sin = _axial_rope_tables(s)
    qf = _rmsnorm(q.reshape(b, s, N_HEADS, HEAD_DIM), qn_g).astype(jnp.float32)
    kf = _rmsnorm(k.reshape(b, s, N_KV, HEAD_DIM), kn_g).astype(jnp.float32)
    qr = _apply_axial_rope(qf, cos, sin).astype(x.dtype).reshape(b, s, N_KV, GROUP, HEAD_DIM)
    kr = _apply_axial_rope(kf, cos, sin).astype(x.dtype)
    vh = v.reshape(b, s, N_KV, HEAD_DIM)
    a = _block_attention(qr, kr, vh)
    a = a * jax.nn.silu(z_attn)
    y_attn = jnp.einsum('bsa,ad->bsd', a, w_attn_o)

    m = jax.nn.sigmoid(g_conv) * y_conv + jax.nn.sigmoid(g_attn) * y_attn
    out = jnp.einsum('bsd,de->bse', m, w_out)
    return x + _rmsnorm(out, post_g)


def setup_inputs(seed: int = 0) -> dict:
    key = jax.random.key(seed)
    ks = jax.random.split(key, 16)
    f32 = jnp.float32
    nrm = lambda k, shp, sc: jax.random.normal(k, shp, f32) * sc
    return {
        "x_prompt": nrm(ks[0], (BATCH, SEQ, D_MODEL), 1.0),
        "x_sample": nrm(ks[1], (DEC_BATCH, DEC_SEQ, D_MODEL), 1.0),
        "pre_norm_g": 1.0 + nrm(ks[2], (DEPTH, D_MODEL), 0.02),
        "w_in": nrm(ks[3], (DEPTH, D_MODEL, N_IN), D_MODEL ** -0.5),
        "dw_kernel": nrm(ks[4], (DEPTH, CONV_K, 1, D_CONV), CONV_K ** -0.5),
        "dw_bias": nrm(ks[5], (DEPTH, D_CONV), 0.02),
        "conv_ln_g": 1.0 + nrm(ks[6], (DEPTH, D_CONV), 0.02),
        "conv_ln_b": nrm(ks[7], (DEPTH, D_CONV), 0.02),
        "w_conv_out": nrm(ks[8], (DEPTH, D_CONV, D_MODEL), D_CONV ** -0.5),
        "q_norm_g": 1.0 + nrm(ks[9], (DEPTH, HEAD_DIM), 0.02),
        "k_norm_g": 1.0 + nrm(ks[10], (DEPTH, HEAD_DIM), 0.02),
        "w_attn_o": nrm(ks[11], (DEPTH, D_ATTN, D_MODEL), D_ATTN ** -0.5),
        "w_out": nrm(ks[12], (DEPTH, D_MODEL, D_MODEL), D_MODEL ** -0.5),
        "post_norm_g": 1.0 + nrm(ks[13], (DEPTH, D_MODEL), 0.02),
    }


def reference(x_prompt, x_sample, pre_norm_g, w_in, dw_kernel, dw_bias, conv_ln_g, conv_ln_b,
              w_conv_out, q_norm_g, k_norm_g, w_attn_o, w_out, post_norm_g):
    y_prompt = x_prompt
    y_sample = x_sample
    for l in range(DEPTH):
        args = (pre_norm_g[l], w_in[l], dw_kernel[l], dw_bias[l], conv_ln_g[l], conv_ln_b[l],
                w_conv_out[l], q_norm_g[l], k_norm_g[l], w_attn_o[l], w_out[l], post_norm_g[l])
        y_prompt = _layer(y_prompt, *args)
        y_sample = _layer(y_sample, *args)
    return (y_prompt, y_sample)
```

```python
import functools
import math

import jax
import jax.numpy as jnp
from jax import lax
from jax.experimental import pallas as pl
from jax.experimental.pallas import tpu as pltpu

F32 = jnp.float32
BF16 = jnp.bfloat16

D_MODEL = 2048
DEPTH = 4
GRID_W = 64
D_CONV = D_MODEL // 2
CONV_K = 31
CONV_PAD = CONV_K // 2
HEAD_DIM = 128
N_HEADS = D_MODEL // HEAD_DIM
N_KV = 4
GROUP = N_HEADS // N_KV
D_ATTN = N_HEADS * HEAD_DIM
D_KV = N_KV * HEAD_DIM
ROPE_AXIS_DIM = HEAD_DIM // 2
ROPE_FREQS = ROPE_AXIS_DIM // 2
ROPE_THETA = 10000.0
EPS = 1e-6
N_IN = 3 * D_CONV + 2 * D_ATTN + 2 * D_KV + 2 * D_MODEL

OFF_Q = 0
OFF_ZATTN = OFF_Q + D_ATTN
OFF_GCONV = OFF_ZATTN + D_ATTN
OFF_GATTN = OFF_GCONV + D_MODEL
OFF_GLU_A = OFF_GATTN + D_MODEL
OFF_GLU_B = OFF_GLU_A + D_CONV
OFF_ZCONV = OFF_GLU_B + D_CONV
OFF_K = OFF_ZCONV + D_CONV
OFF_V = OFF_K + D_KV

HALO = 16
LANES = 128
VMEM_LIMIT = 56 * 1024 * 1024
NEG_BIG = -1e30
Q_SCALE = (1.0 / math.sqrt(HEAD_DIM)) * math.log2(math.e)


def _sigmoid(x):
    return 1.0 / (1.0 + jnp.exp(-x))


def _silu(x):
    return x * _sigmoid(x)


def _inproj_kernel(x_ref, g_ref, w_ref, o_ref, h_ref):
    @pl.when(pl.program_id(1) == 0)
    def _():
        x = x_ref[...]
        ms = jnp.mean(x * x, axis=-1, keepdims=True)
        h_ref[...] = (x * lax.rsqrt(ms + EPS) * g_ref[...]).astype(BF16)

    o_ref[...] = jnp.dot(h_ref[...], w_ref[...], preferred_element_type=F32).astype(o_ref.dtype)


def _inproj(x, g, w):
    t = x.shape[0]
    tm = min(1024, t)
    tn = 1024
    return pl.pallas_call(
        _inproj_kernel,
        grid=(t // tm, N_IN // tn),
        in_specs=[
            pl.BlockSpec((tm, D_MODEL), lambda i, j: (i, 0)),
            pl.BlockSpec((1, D_MODEL), lambda i, j: (0, 0)),
            pl.BlockSpec((D_MODEL, tn), lambda i, j: (0, j)),
        ],
        out_specs=pl.BlockSpec((tm, tn), lambda i, j: (i, j)),
        out_shape=jax.ShapeDtypeStruct((t, N_IN), BF16),
        scratch_shapes=[pltpu.VMEM((tm, D_MODEL), BF16)],
        compiler_params=pltpu.CompilerParams(
            dimension_semantics=("parallel", "arbitrary"), vmem_limit_bytes=VMEM_LIMIT),
        name="inproj",
    )(x, g, w)


def _norm_rope(x, g, cos, sin):
    ms = jnp.mean(x * x, axis=-1, keepdims=True)
    y = x * lax.rsqrt(ms + EPS) * g
    return y * cos + pltpu.roll(y, HEAD_DIM // 2, axis=1) * sin


def _qkv_prep_kernel(q_ref, k_ref, v_ref, cos_ref, sin_ref, gq_ref, gk_ref,
                     qt_ref, ko_ref, vt_ref):
    cos = cos_ref[...]
    sin = sin_ref[...]
    gq = gq_ref[...]
    gk = gk_ref[...]
    for h in range(N_HEADS):
        hs = slice(h * HEAD_DIM, (h + 1) * HEAD_DIM)
        y = _norm_rope(q_ref[:, hs].astype(F32), gq, cos, sin) * Q_SCALE
        qt_ref[hs, :] = y.T.astype(BF16)
    for h in range(N_KV):
        hs = slice(h * HEAD_DIM, (h + 1) * HEAD_DIM)
        y = _norm_rope(k_ref[:, hs].astype(F32), gk, cos, sin)
        ko_ref[:, hs] = y.astype(BF16)
        vt_ref[hs, :] = v_ref[:, hs].astype(F32).T.astype(BF16)


def _qkv_prep(proj, cos, sin, gq, gk, seq_len):
    t = proj.shape[0]
    tm = min(512, seq_len)
    tiles_per_seq = seq_len // tm
    return pl.pallas_call(
        _qkv_prep_kernel,
        grid=(t // tm,),
        in_specs=[
            pl.BlockSpec((tm, D_ATTN), lambda i: (i, OFF_Q // D_ATTN)),
            pl.BlockSpec((tm, D_KV), lambda i: (i, OFF_K // D_KV)),
            pl.BlockSpec((tm, D_KV), lambda i: (i, OFF_V // D_KV)),
            pl.BlockSpec((tm, HEAD_DIM), lambda i: (i % tiles_per_seq, 0)),
            pl.BlockSpec((tm, HEAD_DIM), lambda i: (i % tiles_per_seq, 0)),
            pl.BlockSpec((1, HEAD_DIM), lambda i: (0, 0)),
            pl.BlockSpec((1, HEAD_DIM), lambda i: (0, 0)),
        ],
        out_specs=[
            pl.BlockSpec((D_ATTN, tm), lambda i: (0, i)),
            pl.BlockSpec((tm, D_KV), lambda i: (i, 0)),
            pl.BlockSpec((D_KV, tm), lambda i: (0, i)),
        ],
        out_shape=[
            jax.ShapeDtypeStruct((D_ATTN, t), BF16),
            jax.ShapeDtypeStruct((t, D_KV), BF16),
            jax.ShapeDtypeStruct((D_KV, t), BF16),
        ],
        compiler_params=pltpu.CompilerParams(
            dimension_semantics=("parallel",), vmem_limit_bytes=VMEM_LIMIT),
        name="qkv_prep",
    )(proj, proj, proj, cos, sin, gq, gk)


def _attention_kernel(qt_ref, k_ref, vt_ref, o_ref, acc_ref, *, tk, n_kv_blocks):
    tq = qt_ref.shape[1]
    acc_ref[...] = jnp.zeros_like(acc_ref)

    def body(j, carry):
        off = pl.multiple_of(j * tk, tk)
        kb = k_ref[pl.ds(off, tk), :]
        vb = vt_ref[:, pl.ds(off, tk)]
        new = []
        for g in range(GROUP):
            m, l = carry[g]
            hs = slice(g * HEAD_DIM, (g + 1) * HEAD_DIM)
            s = jnp.dot(kb, qt_ref[hs, :], preferred_element_type=F32)
            m_new = jnp.maximum(m, jnp.max(s, axis=0, keepdims=True))
            alpha = jnp.exp2(m - m_new)
            p = jnp.exp2(s - m_new)
            l_new = alpha * l + jnp.sum(p, axis=0, keepdims=True)
            pv = jnp.dot(vb, p.astype(BF16), preferred_element_type=F32)
            acc_ref[hs, :] = alpha * acc_ref[hs, :] + pv
            new.append((m_new, l_new))
        return tuple(new)

    init = tuple((jnp.full((1, tq), NEG_BIG, F32), jnp.zeros((1, tq), F32)) for _ in range(GROUP))
    stats = lax.fori_loop(0, n_kv_blocks, body, init)
    for g in range(GROUP):
        hs = slice(g * HEAD_DIM, (g + 1) * HEAD_DIM)
        out = acc_ref[hs, :] / stats[g][1]
        o_ref[:, hs] = out.T.astype(o_ref.dtype)


def _attention(qt, k, vt, n_seq, seq_len):
    t = k.shape[0]
    tq = min(256, seq_len)
    tk = min(512, seq_len)
    q_tiles = seq_len // tq
    gw = GROUP * HEAD_DIM
    return pl.pallas_call(
        functools.partial(_attention_kernel, tk=tk, n_kv_blocks=seq_len // tk),
        grid=(n_seq, N_KV, q_tiles),
        in_specs=[
            pl.BlockSpec((gw, tq), lambda b, h, i: (h, b * q_tiles + i)),
            pl.BlockSpec((seq_len, HEAD_DIM), lambda b, h, i: (b, h)),
            pl.BlockSpec((HEAD_DIM, seq_len), lambda b, h, i: (h, b)),
        ],
        out_specs=pl.BlockSpec((tq, gw), lambda b, h, i: (b * q_tiles + i, h)),
        out_shape=jax.ShapeDtypeStruct((t, D_ATTN), BF16),
        scratch_shapes=[pltpu.VMEM((gw, tq), F32)],
        compiler_params=pltpu.CompilerParams(
            dimension_semantics=("parallel", "parallel", "arbitrary"), vmem_limit_bytes=VMEM_LIMIT),
        name="attention",
    )(qt, k, vt)


def _conv_kernel(a_ref, b_ref, z_ref, ap_ref, bp_ref, an_ref, bn_ref,
                 w_ref, bias_ref, lng_ref, lnb_ref, o_ref, ext_ref, cv_ref, *, tiles_per_seq):
    tm = a_ref.shape[0]
    pos = pl.program_id(0) % tiles_per_seq

    def glu(a, b):
        return a.astype(F32) * _sigmoid(b.astype(F32))

    ext_ref[0:HALO, :] = jnp.where(pos != 0, glu(ap_ref[...], bp_ref[...]), 0.0)
    ext_ref[HALO:HALO + tm, :] = glu(a_ref[...], b_ref[...])
    ext_ref[HALO + tm:2 * HALO + tm, :] = jnp.where(
        pos != tiles_per_seq - 1, glu(an_ref[...], bn_ref[...]), 0.0)

    base = HALO - CONV_PAD
    for c in range(D_CONV // LANES):
        cs = slice(c * LANES, (c + 1) * LANES)
        acc = jnp.zeros((tm, LANES), F32)
        for k in range(CONV_K):
            acc = acc + ext_ref[base + k:base + k + tm, cs] * w_ref[k:k + 1, cs]
        cv_ref[:, cs] = acc + bias_ref[:, cs]

    u = cv_ref[...]
    mu = jnp.mean(u, axis=-1, keepdims=True)
    uc = u - mu
    var = jnp.mean(uc * uc, axis=-1, keepdims=True)
    y = uc * lax.rsqrt(var + EPS) * lng_ref[...] + lnb_ref[...]
    o_ref[...] = (_silu(y) * _silu(z_ref[...].astype(F32))).astype(o_ref.dtype)


def _conv_branch(proj, dw, bias, lng, lnb, seq_len):
    t = proj.shape[0]
    tm = min(128, seq_len)
    tiles_per_seq = seq_len // tm
    hb = tm // HALO
    n_hb = t // HALO
    ca, cb, cz = OFF_GLU_A // D_CONV, OFF_GLU_B // D_CONV, OFF_ZCONV // D_CONV
    prev_map = lambda c: (lambda i: (jnp.maximum(i * hb - 1, 0), c))
    next_map = lambda c: (lambda i: (jnp.minimum((i + 1) * hb, n_hb - 1), c))
    row = lambda i: (0, 0)
    return pl.pallas_call(
        functools.partial(_conv_kernel, tiles_per_seq=tiles_per_seq),
        grid=(t // tm,),
        in_specs=[
            pl.BlockSpec((tm, D_CONV), lambda i: (i, ca)),
            pl.BlockSpec((tm, D_CONV), lambda i: (i, cb)),
            pl.BlockSpec((tm, D_CONV), lambda i: (i, cz)),
            pl.BlockSpec((HALO, D_CONV), prev_map(ca)),
            pl.BlockSpec((HALO, D_CONV), prev_map(cb)),
            pl.BlockSpec((HALO, D_CONV), next_map(ca)),
            pl.BlockSpec((HALO, D_CONV), next_map(cb)),
            pl.BlockSpec((CONV_K, D_CONV), row),
            pl.BlockSpec((1, D_CONV), row),
            pl.BlockSpec((1, D_CONV), row),
            pl.BlockSpec((1, D_CONV), row),
        ],
        out_specs=pl.BlockSpec((tm, D_CONV), lambda i: (i, 0)),
        out_shape=jax.ShapeDtypeStruct((t, D_CONV), BF16),
        scratch_shapes=[pltpu.VMEM((tm + 2 * HALO, D_CONV), F32), pltpu.VMEM((tm, D_CONV), F32)],
        compiler_params=pltpu.CompilerParams(
            dimension_semantics=("parallel",), vmem_limit_bytes=VMEM_LIMIT),
        name="conv_branch",
    )(proj, proj, proj, proj, proj, proj, proj, dw, bias, lng, lnb)


def _merge_kernel(u_ref, a_ref, z_ref, gc_ref, ga_ref, x_ref, wc_ref, wa_ref, wo_ref, pg_ref, o_ref):
    y_conv = jnp.dot(u_ref[...], wc_ref[...], preferred_element_type=F32)
    ag = (a_ref[...].astype(F32) * _silu(z_ref[...].astype(F32))).astype(BF16)
    y_attn = jnp.dot(ag, wa_ref[...], preferred_element_type=F32)
    m = _sigmoid(gc_ref[...].astype(F32)) * y_conv + _sigmoid(ga_ref[...].astype(F32)) * y_attn
    out = jnp.dot(m.astype(BF16), wo_ref[...], preferred_element_type=F32)
    ms = jnp.mean(out * out, axis=-1, keepdims=True)
    o_ref[...] = x_ref[...] + out * lax.rsqrt(ms + EPS) * pg_ref[...]


def _merge_out(u, a, proj, x, wc, wa, wo, pg):
    t = x.shape[0]
    tm = min(256, t)
    const = lambda i: (0, 0)
    resident = functools.partial(pl.BlockSpec, index_map=const, pipeline_mode=pl.Buffered(1))
    return pl.pallas_call(
        _merge_kernel,
        grid=(t // tm,),
        in_specs=[
            pl.BlockSpec((tm, D_CONV), lambda i: (i, 0)),
            pl.BlockSpec((tm, D_ATTN), lambda i: (i, 0)),
            pl.BlockSpec((tm, D_ATTN), lambda i: (i, OFF_ZATTN // D_ATTN)),
            pl.BlockSpec((tm, D_MODEL), lambda i: (i, OFF_GCONV // D_MODEL)),
            pl.BlockSpec((tm, D_MODEL), lambda i: (i, OFF_GATTN // D_MODEL)),
            pl.BlockSpec((tm, D_MODEL), lambda i: (i, 0)),
            resident((D_CONV, D_MODEL)),
            resident((D_ATTN, D_MODEL)),
            resident((D_MODEL, D_MODEL)),
            pl.BlockSpec((1, D_MODEL), const),
        ],
        out_specs=pl.BlockSpec((tm, D_MODEL), lambda i: (i, 0)),
        out_shape=jax.ShapeDtypeStruct((t, D_MODEL), F32),
        compiler_params=pltpu.CompilerParams(
            dimension_semantics=("parallel",), vmem_limit_bytes=VMEM_LIMIT),
        name="merge_out",
    )(u, a, proj, proj, proj, x, wc, wa, wo, pg)


def _pair_major(w):
    lead = w.shape[:-1]
    n = w.shape[-1] // HEAD_DIM
    w = w.reshape(lead + (n, 2, 2, ROPE_FREQS))
    w = jnp.swapaxes(w, -2, -3)
    return w.reshape(lead + (n * HEAD_DIM,))


def _pack_w_in(w_in):
    o = [0, D_CONV, 2 * D_CONV, 3 * D_CONV]
    o.append(o[-1] + D_ATTN)
    o.append(o[-1] + D_KV)
    o.append(o[-1] + D_KV)
    o.append(o[-1] + D_ATTN)
    o.append(o[-1] + D_MODEL)
    o.append(o[-1] + D_MODEL)
    glu_a, glu_b, z_conv, q, k, v, z_attn, g_conv, g_attn = (
        w_in[..., o[n]:o[n + 1]] for n in range(9))
    packed = jnp.concatenate(
        [_pair_major(q), z_attn, g_conv, g_attn, glu_a, glu_b, z_conv, _pair_major(k), v], axis=-1)
    return packed.astype(BF16)


def _rope_tables(seq_len):
    rows = seq_len // GRID_W
    row_idx = jnp.broadcast_to(jnp.arange(rows, dtype=F32)[:, None], (rows, GRID_W)).reshape(-1)
    col_idx = jnp.broadcast_to(jnp.arange(GRID_W, dtype=F32)[None, :], (rows, GRID_W)).reshape(-1)
    inv_freq = ROPE_THETA ** (-(jnp.arange(ROPE_FREQS, dtype=F32) * 2.0) / ROPE_AXIS_DIM)
    ar = row_idx[:, None] * inv_freq
    ac = col_idx[:, None] * inv_freq
    cos = jnp.concatenate([jnp.cos(ar), jnp.cos(ac), jnp.cos(ar), jnp.cos(ac)], axis=-1)
    sin = jnp.concatenate([-jnp.sin(ar), -jnp.sin(ac), jnp.sin(ar), jnp.sin(ac)], axis=-1)
    return cos, sin


def _layer(x, n_seq, seq_len, tables, p):
    cos, sin = tables
    proj = _inproj(x, p["pre_g"], p["w_in"])
    qt, k, vt = _qkv_prep(proj, cos, sin, p["gq"], p["gk"], seq_len)
    a = _attention(qt, k, vt, n_seq, seq_len)
    u = _conv_branch(proj, p["dw"], p["dw_b"], p["ln_g"], p["ln_b"], seq_len)
    return _merge_out(u, a, proj, x, p["w_conv_out"], p["w_attn_o"], p["w_out"], p["post_g"])


def kernel(x_prompt, x_sample, pre_norm_g, w_in, dw_kernel, dw_bias, conv_ln_g, conv_ln_b,
           w_conv_out, q_norm_g, k_norm_g, w_attn_o, w_out, post_norm_g):
    depth = w_in.shape[0]
    w_in_p = _pack_w_in(w_in)
    wc = w_conv_out.astype(BF16)
    wa = w_attn_o.astype(BF16)
    wo = w_out.astype(BF16)
    gq = _pair_major(q_norm_g)
    gk = _pair_major(k_norm_g)

    streams = []
    for x in (x_prompt, x_sample):
        b, s, d = x.shape
        streams.append((x.reshape(b * s, d), b, s, _rope_tables(s)))

    outs = []
    for x2, b, s, tables in streams:
        y = x2
        for l in range(depth):
            p = dict(
                pre_g=pre_norm_g[l][None, :], w_in=w_in_p[l], dw=dw_kernel[l, :, 0, :],
                dw_b=dw_bias[l][None, :], ln_g=conv_ln_g[l][None, :], ln_b=conv_ln_b[l][None, :],
                w_conv_out=wc[l], gq=gq[l][None, :], gk=gk[l][None, :], w_attn_o=wa[l],
                w_out=wo[l], post_g=post_norm_g[l][None, :])
            y = _layer(y, b, s, tables, p)
        outs.append(y.reshape(b, s, d))
    return tuple(outs)
```

```python
import functools
import math

import jax
import jax.numpy as jnp
from jax import lax
from jax.experimental import pallas as pl
from jax.experimental.pallas import tpu as pltpu

F32 = jnp.float32
BF16 = jnp.bfloat16

D_MODEL = 2048
DEPTH = 4
GRID_W = 64
D_CONV = D_MODEL // 2
CONV_K = 31
CONV_PAD = CONV_K // 2
HEAD_DIM = 128
N_HEADS = D_MODEL // HEAD_DIM
N_KV = 4
GROUP = N_HEADS // N_KV
D_ATTN = N_HEADS * HEAD_DIM
D_KV = N_KV * HEAD_DIM
ROPE_AXIS_DIM = HEAD_DIM // 2
ROPE_FREQS = ROPE_AXIS_DIM // 2
ROPE_THETA = 10000.0
EPS = 1e-6
N_IN = 3 * D_CONV + 2 * D_ATTN + 2 * D_KV + 2 * D_MODEL

OFF_Q = 0
OFF_ZATTN = OFF_Q + D_ATTN
OFF_GCONV = OFF_ZATTN + D_ATTN
OFF_GATTN = OFF_GCONV + D_MODEL
OFF_GLU_A = OFF_GATTN + D_MODEL
OFF_GLU_B = OFF_GLU_A + D_CONV
OFF_ZCONV = OFF_GLU_B + D_CONV
OFF_K = OFF_ZCONV + D_CONV
OFF_V = OFF_K + D_KV

HALO = 16
LANES = 128
VMEM_LIMIT = 56 * 1024 * 1024
NEG_BIG = -1e30
MAX_UNSHIFTED = 60.0
Q_SCALE = (1.0 / math.sqrt(HEAD_DIM)) * math.log2(math.e)


def _sigmoid(x):
    return 1.0 / (1.0 + jnp.exp(-x))


def _silu(x):
    return x * _sigmoid(x)


def _inproj_kernel(x_ref, g_ref, w_ref, o_ref, h_ref):
    @pl.when(pl.program_id(1) == 0)
    def _():
        x = x_ref[...]
        ms = jnp.mean(x * x, axis=-1, keepdims=True)
        h_ref[...] = (x * lax.rsqrt(ms + EPS) * g_ref[...]).astype(BF16)

    o_ref[...] = jnp.dot(h_ref[...], w_ref[...], preferred_element_type=F32).astype(o_ref.dtype)


def _inproj(x, g, w):
    t = x.shape[0]
    tm = min(1024, t)
    tn = 1024
    return pl.pallas_call(
        _inproj_kernel,
        grid=(t // tm, N_IN // tn),
        in_specs=[
            pl.BlockSpec((tm, D_MODEL), lambda i, j: (i, 0)),
            pl.BlockSpec((1, D_MODEL), lambda i, j: (0, 0)),
            pl.BlockSpec((D_MODEL, tn), lambda i, j: (0, j)),
        ],
        out_specs=pl.BlockSpec((tm, tn), lambda i, j: (i, j)),
        out_shape=jax.ShapeDtypeStruct((t, N_IN), BF16),
        scratch_shapes=[pltpu.VMEM((tm, D_MODEL), BF16)],
        compiler_params=pltpu.CompilerParams(
            dimension_semantics=("parallel", "arbitrary"), vmem_limit_bytes=VMEM_LIMIT),
        name="inproj",
    )(x, g, w)


def _norm_rope(x, g, cos, sin):
    ms = jnp.mean(x * x, axis=-1, keepdims=True)
    y = x * lax.rsqrt(ms + EPS) * g
    return y * cos + pltpu.roll(y, HEAD_DIM // 2, axis=1) * sin


def _qkv_prep_kernel(q_ref, k_ref, v_ref, cos_ref, sin_ref, gq_ref, gk_ref,
                     qt_ref, ko_ref, vt_ref):
    cos = cos_ref[...]
    sin = sin_ref[...]
    gq = gq_ref[...]
    gk = gk_ref[...]
    for h in range(N_HEADS):
        hs = slice(h * HEAD_DIM, (h + 1) * HEAD_DIM)
        y = _norm_rope(q_ref[:, hs].astype(F32), gq, cos, sin) * Q_SCALE
        qt_ref[hs, :] = y.T.astype(BF16)
    for h in range(N_KV):
        hs = slice(h * HEAD_DIM, (h + 1) * HEAD_DIM)
        y = _norm_rope(k_ref[:, hs].astype(F32), gk, cos, sin)
        ko_ref[:, hs] = y.astype(BF16)
        vt_ref[hs, :] = v_ref[:, hs].astype(F32).T.astype(BF16)


def _qkv_prep(proj, cos, sin, gq, gk, seq_len):
    t = proj.shape[0]
    tm = min(512, seq_len)
    tiles_per_seq = seq_len // tm
    return pl.pallas_call(
        _qkv_prep_kernel,
        grid=(t // tm,),
        in_specs=[
            pl.BlockSpec((tm, D_ATTN), lambda i: (i, OFF_Q // D_ATTN)),
            pl.BlockSpec((tm, D_KV), lambda i: (i, OFF_K // D_KV)),
            pl.BlockSpec((tm, D_KV), lambda i: (i, OFF_V // D_KV)),
            pl.BlockSpec((tm, HEAD_DIM), lambda i: (i % tiles_per_seq, 0)),
            pl.BlockSpec((tm, HEAD_DIM), lambda i: (i % tiles_per_seq, 0)),
            pl.BlockSpec((1, HEAD_DIM), lambda i: (0, 0)),
            pl.BlockSpec((1, HEAD_DIM), lambda i: (0, 0)),
        ],
        out_specs=[
            pl.BlockSpec((D_ATTN, tm), lambda i: (0, i)),
            pl.BlockSpec((tm, D_KV), lambda i: (i, 0)),
            pl.BlockSpec((D_KV, tm), lambda i: (0, i)),
        ],
        out_shape=[
            jax.ShapeDtypeStruct((D_ATTN, t), BF16),
            jax.ShapeDtypeStruct((t, D_KV), BF16),
            jax.ShapeDtypeStruct((D_KV, t), BF16),
        ],
        compiler_params=pltpu.CompilerParams(
            dimension_semantics=("parallel",), vmem_limit_bytes=VMEM_LIMIT),
        name="qkv_prep",
    )(proj, proj, proj, cos, sin, gq, gk)


def _attention_kernel(qt_ref, k_ref, vt_ref, o_ref, acc_ref, *, tk, n_kv_blocks):
    tq = qt_ref.shape[1]
    acc_ref[...] = jnp.zeros_like(acc_ref)

    def body(j, carry):
        off = pl.multiple_of(j * tk, tk)
        kb = k_ref[pl.ds(off, tk), :]
        vb = vt_ref[:, pl.ds(off, tk)]
        new = []
        for g in range(GROUP):
            m, l = carry[g]
            hs = slice(g * HEAD_DIM, (g + 1) * HEAD_DIM)
            s = jnp.dot(kb, qt_ref[hs, :], preferred_element_type=F32)
            m_new = jnp.maximum(m, jnp.max(s, axis=0, keepdims=True))
            alpha = jnp.exp2(m - m_new)
            p = jnp.exp2(s - m_new)
            l_new = alpha * l + jnp.sum(p, axis=0, keepdims=True)
            pv = jnp.dot(vb, p.astype(BF16), preferred_element_type=F32)
            acc_ref[hs, :] = alpha * acc_ref[hs, :] + pv
            new.append((m_new, l_new))
        return tuple(new)

    init = tuple((jnp.full((1, tq), NEG_BIG, F32), jnp.zeros((1, tq), F32)) for _ in range(GROUP))
    stats = lax.fori_loop(0, n_kv_blocks, body, init)
    for g in range(GROUP):
        hs = slice(g * HEAD_DIM, (g + 1) * HEAD_DIM)
        out = acc_ref[hs, :] / stats[g][1]
        o_ref[:, hs] = out.T.astype(o_ref.dtype)


def _attention_unshifted_kernel(qt_ref, k_ref, vt_ref, o_ref, acc_ref, s0_ref, *, tk, n_kv_blocks):
    tq = qt_ref.shape[1]
    heads = [slice(g * HEAD_DIM, (g + 1) * HEAD_DIM) for g in range(GROUP)]
    acc_ref[...] = jnp.zeros_like(acc_ref)

    def scores(j, g):
        off = pl.multiple_of(j * tk, tk)
        return jnp.dot(k_ref[pl.ds(off, tk), :], qt_ref[heads[g], :], preferred_element_type=F32)

    s0_ref[...] = scores(0, 0)

    def body(j, carry):
        off = pl.multiple_of(j * tk, tk)
        vb = vt_ref[:, pl.ds(off, tk)]
        new = []
        s = s0_ref[...]
        for g in range(GROUP):
            if g + 1 < GROUP:
                s_next = scores(j, g + 1)
            else:
                s_next = scores(jnp.minimum(j + 1, n_kv_blocks - 1), 0)
            p = jnp.exp2(s)
            new.append(carry[g] + jnp.sum(p, axis=0, keepdims=True))
            acc_ref[heads[g], :] += jnp.dot(vb, p.astype(BF16), preferred_element_type=F32)
            s = s_next
        s0_ref[...] = s
        return tuple(new)

    init = tuple(jnp.zeros((1, tq), F32) for _ in range(GROUP))
    l = lax.fori_loop(0, n_kv_blocks, body, init)
    for g in range(GROUP):
        hs = slice(g * HEAD_DIM, (g + 1) * HEAD_DIM)
        out = acc_ref[hs, :] / l[g]
        o_ref[:, hs] = out.T.astype(o_ref.dtype)


def _attention(qt, k, vt, n_seq, seq_len, body):
    t = k.shape[0]
    tq = min(256, seq_len)
    unshifted = body is _attention_unshifted_kernel
    tk = min(2048 if unshifted else 512, seq_len)
    q_tiles = seq_len // tq
    gw = GROUP * HEAD_DIM
    scratch = [pltpu.VMEM((gw, tq), F32)]
    if unshifted:
        scratch.append(pltpu.VMEM((tk, tq), F32))
    return pl.pallas_call(
        functools.partial(body, tk=tk, n_kv_blocks=seq_len // tk),
        grid=(n_seq, N_KV, q_tiles),
        in_specs=[
            pl.BlockSpec((gw, tq), lambda b, h, i: (h, b * q_tiles + i)),
            pl.BlockSpec((seq_len, HEAD_DIM), lambda b, h, i: (b, h)),
            pl.BlockSpec((HEAD_DIM, seq_len), lambda b, h, i: (h, b)),
        ],
        out_specs=pl.BlockSpec((tq, gw), lambda b, h, i: (b * q_tiles + i, h)),
        out_shape=jax.ShapeDtypeStruct((t, D_ATTN), BF16),
        scratch_shapes=scratch,
        compiler_params=pltpu.CompilerParams(
            dimension_semantics=("parallel", "parallel", "arbitrary"), vmem_limit_bytes=VMEM_LIMIT),
        name="attention_unshifted" if unshifted else "attention_online",
    )(qt, k, vt)


def _conv_kernel(a_ref, b_ref, z_ref, ap_ref, bp_ref, an_ref, bn_ref,
                 w_ref, bias_ref, lng_ref, lnb_ref, o_ref, ext_ref, cv_ref, *, tiles_per_seq):
    tm = a_ref.shape[0]
    pos = pl.program_id(0) % tiles_per_seq

    def glu(a, b):
        return a.astype(F32) * _sigmoid(b.astype(F32))

    ext_ref[0:HALO, :] = jnp.where(pos != 0, glu(ap_ref[...], bp_ref[...]), 0.0)
    ext_ref[HALO:HALO + tm, :] = glu(a_ref[...], b_ref[...])
    ext_ref[HALO + tm:2 * HALO + tm, :] = jnp.where(
        pos != tiles_per_seq - 1, glu(an_ref[...], bn_ref[...]), 0.0)

    base = HALO - CONV_PAD
    for c in range(D_CONV // LANES):
        cs = slice(c * LANES, (c + 1) * LANES)
        acc = jnp.zeros((tm, LANES), F32)
        for k in range(CONV_K):
            acc = acc + ext_ref[base + k:base + k + tm, cs] * w_ref[k:k + 1, cs]
        cv_ref[:, cs] = acc + bias_ref[:, cs]

    u = cv_ref[...]
    mu = jnp.mean(u, axis=-1, keepdims=True)
    uc = u - mu
    var = jnp.mean(uc * uc, axis=-1, keepdims=True)
    y = uc * lax.rsqrt(var + EPS) * lng_ref[...] + lnb_ref[...]
    o_ref[...] = (_silu(y) * _silu(z_ref[...].astype(F32))).astype(o_ref.dtype)


def _conv_branch(proj, dw, bias, lng, lnb, seq_len):
    t = proj.shape[0]
    tm = min(128, seq_len)
    tiles_per_seq = seq_len // tm
    hb = tm // HALO
    n_hb = t // HALO
    ca, cb, cz = OFF_GLU_A // D_CONV, OFF_GLU_B // D_CONV, OFF_ZCONV // D_CONV
    prev_map = lambda c: (lambda i: (jnp.maximum(i * hb - 1, 0), c))
    next_map = lambda c: (lambda i: (jnp.minimum((i + 1) * hb, n_hb - 1), c))
    row = lambda i: (0, 0)
    return pl.pallas_call(
        functools.partial(_conv_kernel, tiles_per_seq=tiles_per_seq),
        grid=(t // tm,),
        in_specs=[
            pl.BlockSpec((tm, D_CONV), lambda i: (i, ca)),
            pl.BlockSpec((tm, D_CONV), lambda i: (i, cb)),
            pl.BlockSpec((tm, D_CONV), lambda i: (i, cz)),
            pl.BlockSpec((HALO, D_CONV), prev_map(ca)),
            pl.BlockSpec((HALO, D_CONV), prev_map(cb)),
            pl.BlockSpec((HALO, D_CONV), next_map(ca)),
            pl.BlockSpec((HALO, D_CONV), next_map(cb)),
            pl.BlockSpec((CONV_K, D_CONV), row),
            pl.BlockSpec((1, D_CONV), row),
            pl.BlockSpec((1, D_CONV), row),
            pl.BlockSpec((1, D_CONV), row),
        ],
        out_specs=pl.BlockSpec((tm, D_CONV), lambda i: (i, 0)),
        out_shape=jax.ShapeDtypeStruct((t, D_CONV), BF16),
        scratch_shapes=[pltpu.VMEM((tm + 2 * HALO, D_CONV), F32), pltpu.VMEM((tm, D_CONV), F32)],
        compiler_params=pltpu.CompilerParams(
            dimension_semantics=("parallel",), vmem_limit_bytes=VMEM_LIMIT),
        name="conv_branch",
    )(proj, proj, proj, proj, proj, proj, proj, dw, bias, lng, lnb)


def _merge_kernel(u_ref, a_ref, z_ref, gc_ref, ga_ref, x_ref, wc_ref, wa_ref, wo_ref, pg_ref, o_ref):
    y_conv = jnp.dot(u_ref[...], wc_ref[...], preferred_element_type=F32)
    ag = (a_ref[...].astype(F32) * _silu(z_ref[...].astype(F32))).astype(BF16)
    y_attn = jnp.dot(ag, wa_ref[...], preferred_element_type=F32)
    m = _sigmoid(gc_ref[...].astype(F32)) * y_conv + _sigmoid(ga_ref[...].astype(F32)) * y_attn
    out = jnp.dot(m.astype(BF16), wo_ref[...], preferred_element_type=F32)
    ms = jnp.mean(out * out, axis=-1, keepdims=True)
    o_ref[...] = x_ref[...] + out * lax.rsqrt(ms + EPS) * pg_ref[...]


def _merge_out(u, a, proj, x, wc, wa, wo, pg):
    t = x.shape[0]
    tm = min(256, t)
    const = lambda i: (0, 0)
    resident = functools.partial(pl.BlockSpec, index_map=const, pipeline_mode=pl.Buffered(1))
    return pl.pallas_call(
        _merge_kernel,
        grid=(t // tm,),
        in_specs=[
            pl.BlockSpec((tm, D_CONV), lambda i: (i, 0)),
            pl.BlockSpec((tm, D_ATTN), lambda i: (i, 0)),
            pl.BlockSpec((tm, D_ATTN), lambda i: (i, OFF_ZATTN // D_ATTN)),
            pl.BlockSpec((tm, D_MODEL), lambda i: (i, OFF_GCONV // D_MODEL)),
            pl.BlockSpec((tm, D_MODEL), lambda i: (i, OFF_GATTN // D_MODEL)),
            pl.BlockSpec((tm, D_MODEL), lambda i: (i, 0)),
            resident((D_CONV, D_MODEL)),
            resident((D_ATTN, D_MODEL)),
            resident((D_MODEL, D_MODEL)),
            pl.BlockSpec((1, D_MODEL), const),
        ],
        out_specs=pl.BlockSpec((tm, D_MODEL), lambda i: (i, 0)),
        out_shape=jax.ShapeDtypeStruct((t, D_MODEL), F32),
        compiler_params=pltpu.CompilerParams(
            dimension_semantics=("parallel",), vmem_limit_bytes=VMEM_LIMIT),
        name="merge_out",
    )(u, a, proj, proj, proj, x, wc, wa, wo, pg)


def _pair_major(w):
    lead = w.shape[:-1]
    n = w.shape[-1] // HEAD_DIM
    w = w.reshape(lead + (n, 2, 2, ROPE_FREQS))
    w = jnp.swapaxes(w, -2, -3)
    return w.reshape(lead + (n * HEAD_DIM,))


def _pack_w_in(w_in):
    o = [0, D_CONV, 2 * D_CONV, 3 * D_CONV]
    o.append(o[-1] + D_ATTN)
    o.append(o[-1] + D_KV)
    o.append(o[-1] + D_KV)
    o.append(o[-1] + D_ATTN)
    o.append(o[-1] + D_MODEL)
    o.append(o[-1] + D_MODEL)
    glu_a, glu_b, z_conv, q, k, v, z_attn, g_conv, g_attn = (
        w_in[..., o[n]:o[n + 1]] for n in range(9))
    packed = jnp.concatenate(
        [_pair_major(q), z_attn, g_conv, g_attn, glu_a, glu_b, z_conv, _pair_major(k), v], axis=-1)
    return packed.astype(BF16)


def _rope_tables(seq_len):
    rows = seq_len // GRID_W
    row_idx = jnp.broadcast_to(jnp.arange(rows, dtype=F32)[:, None], (rows, GRID_W)).reshape(-1)
    col_idx = jnp.broadcast_to(jnp.arange(GRID_W, dtype=F32)[None, :], (rows, GRID_W)).reshape(-1)
    inv_freq = ROPE_THETA ** (-(jnp.arange(ROPE_FREQS, dtype=F32) * 2.0) / ROPE_AXIS_DIM)
    ar = row_idx[:, None] * inv_freq
    ac = col_idx[:, None] * inv_freq
    cos = jnp.concatenate([jnp.cos(ar), jnp.cos(ac), jnp.cos(ar), jnp.cos(ac)], axis=-1)
    sin = jnp.concatenate([-jnp.sin(ar), -jnp.sin(ac), jnp.sin(ar), jnp.sin(ac)], axis=-1)
    return cos, sin


def _layer(x, n_seq, seq_len, tables, p):
    cos, sin = tables
    proj = _inproj(x, p["pre_g"], p["w_in"])
    qt, k, vt = _qkv_prep(proj, cos, sin, p["gq"], p["gk"], seq_len)
    score_bound = HEAD_DIM * Q_SCALE * jnp.max(jnp.abs(p["gq"])) * jnp.max(jnp.abs(p["gk"]))
    a = lax.cond(
        score_bound <= MAX_UNSHIFTED,
        functools.partial(_attention, n_seq=n_seq, seq_len=seq_len, body=_attention_unshifted_kernel),
        functools.partial(_attention, n_seq=n_seq, seq_len=seq_len, body=_attention_kernel),
        qt, k, vt)
    u = _conv_branch(proj, p["dw"], p["dw_b"], p["ln_g"], p["ln_b"], seq_len)
    return _merge_out(u, a, proj, x, p["w_conv_out"], p["w_attn_o"], p["w_out"], p["post_g"])


def kernel(x_prompt, x_sample, pre_norm_g, w_in, dw_kernel, dw_bias, conv_ln_g, conv_ln_b,
           w_conv_out, q_norm_g, k_norm_g, w_attn_o, w_out, post_norm_g):
    depth = w_in.shape[0]
    w_in_p = _pack_w_in(w_in)
    wc = w_conv_out.astype(BF16)
    wa = w_attn_o.astype(BF16)
    wo = w_out.astype(BF16)
    gq = _pair_major(q_norm_g)
    gk = _pair_major(k_norm_g)

    streams = []
    for x in (x_prompt, x_sample):
        b, s, d = x.shape
        streams.append((x.reshape(b * s, d), b, s, _rope_tables(s)))

    outs = []
    for x2, b, s, tables in streams:
        y = x2
        for l in range(depth):
            p = dict(
                pre_g=pre_norm_g[l][None, :], w_in=w_in_p[l], dw=dw_kernel[l, :, 0, :],
                dw_b=dw_bias[l][None, :], ln_g=conv_ln_g[l][None, :], ln_b=conv_ln_b[l][None, :],
                w_conv_out=wc[l], gq=gq[l][None, :], gk=gk[l][None, :], w_attn_o=wa[l],
                w_out=wo[l], post_g=post_norm_g[l][None, :])
            y = _layer(y, b, s, tables, p)
        outs.append(y.reshape(b, s, d))
    return tuple(outs)
```

```python
import functools
import math

import jax
import jax.numpy as jnp
from jax import lax
from jax.experimental import pallas as pl
from jax.experimental.pallas import tpu as pltpu

F32 = jnp.float32
BF16 = jnp.bfloat16

D_MODEL = 2048
DEPTH = 4
GRID_W = 64
D_CONV = D_MODEL // 2
CONV_K = 31
CONV_PAD = CONV_K // 2
HEAD_DIM = 128
N_HEADS = D_MODEL // HEAD_DIM
N_KV = 4
GROUP = N_HEADS // N_KV
D_ATTN = N_HEADS * HEAD_DIM
D_KV = N_KV * HEAD_DIM
ROPE_AXIS_DIM = HEAD_DIM // 2
ROPE_FREQS = ROPE_AXIS_DIM // 2
ROPE_THETA = 10000.0
EPS = 1e-6
N_IN = 3 * D_CONV + 2 * D_ATTN + 2 * D_KV + 2 * D_MODEL

OFF_Q = 0
OFF_ZATTN = OFF_Q + D_ATTN
OFF_GCONV = OFF_ZATTN + D_ATTN
OFF_GATTN = OFF_GCONV + D_MODEL
OFF_GLU_A = OFF_GATTN + D_MODEL
OFF_GLU_B = OFF_GLU_A + D_CONV
OFF_ZCONV = OFF_GLU_B + D_CONV
OFF_K = OFF_ZCONV + D_CONV
OFF_V = OFF_K + D_KV

HALO = 16
LANES = 128
SUBLANES = 8
SHIFT_SPAN = (HALO - CONV_PAD + CONV_K - 1) // SUBLANES * SUBLANES
VMEM_LIMIT = 56 * 1024 * 1024
NEG_BIG = -1e30
MAX_UNSHIFTED = 60.0
Q_SCALE = (1.0 / math.sqrt(HEAD_DIM)) * math.log2(math.e)


def _sigmoid(x):
    return 1.0 / (1.0 + jnp.exp(-x))


def _silu(x):
    return x * _sigmoid(x)


def _inproj_kernel(x_ref, g_ref, w_ref, o_ref, h_ref):
    @pl.when(pl.program_id(1) == 0)
    def _():
        x = x_ref[...]
        ms = jnp.mean(x * x, axis=-1, keepdims=True)
        h_ref[...] = (x * lax.rsqrt(ms + EPS) * g_ref[...]).astype(BF16)

    o_ref[...] = jnp.dot(h_ref[...], w_ref[...], preferred_element_type=F32).astype(o_ref.dtype)


def _inproj(x, g, w):
    t = x.shape[0]
    tm = min(1024, t)
    tn = 1024
    return pl.pallas_call(
        _inproj_kernel,
        grid=(t // tm, N_IN // tn),
        in_specs=[
            pl.BlockSpec((tm, D_MODEL), lambda i, j: (i, 0)),
            pl.BlockSpec((1, D_MODEL), lambda i, j: (0, 0)),
            pl.BlockSpec((D_MODEL, tn), lambda i, j: (0, j)),
        ],
        out_specs=pl.BlockSpec((tm, tn), lambda i, j: (i, j)),
        out_shape=jax.ShapeDtypeStruct((t, N_IN), BF16),
        scratch_shapes=[pltpu.VMEM((tm, D_MODEL), BF16)],
        compiler_params=pltpu.CompilerParams(
            dimension_semantics=("parallel", "arbitrary"), vmem_limit_bytes=VMEM_LIMIT),
        name="inproj",
    )(x, g, w)


def _norm_rope(x, g, cos, sin):
    ms = jnp.mean(x * x, axis=-1, keepdims=True)
    y = x * lax.rsqrt(ms + EPS) * g
    return y * cos + pltpu.roll(y, HEAD_DIM // 2, axis=1) * sin


def _qkv_prep_kernel(q_ref, k_ref, v_ref, cos_ref, sin_ref, gq_ref, gk_ref,
                     qt_ref, ko_ref, vt_ref):
    cos = cos_ref[...]
    sin = sin_ref[...]
    gq = gq_ref[...]
    gk = gk_ref[...]
    for h in range(N_HEADS):
        hs = slice(h * HEAD_DIM, (h + 1) * HEAD_DIM)
        y = _norm_rope(q_ref[:, hs].astype(F32), gq, cos, sin) * Q_SCALE
        qt_ref[hs, :] = y.T.astype(BF16)
    for h in range(N_KV):
        hs = slice(h * HEAD_DIM, (h + 1) * HEAD_DIM)
        y = _norm_rope(k_ref[:, hs].astype(F32), gk, cos, sin)
        ko_ref[:, hs] = y.astype(BF16)
        vt_ref[hs, :] = v_ref[:, hs].astype(F32).T.astype(BF16)


def _qkv_prep(proj, cos, sin, gq, gk, seq_len):
    t = proj.shape[0]
    tm = min(512, seq_len)
    tiles_per_seq = seq_len // tm
    return pl.pallas_call(
        _qkv_prep_kernel,
        grid=(t // tm,),
        in_specs=[
            pl.BlockSpec((tm, D_ATTN), lambda i: (i, OFF_Q // D_ATTN)),
            pl.BlockSpec((tm, D_KV), lambda i: (i, OFF_K // D_KV)),
            pl.BlockSpec((tm, D_KV), lambda i: (i, OFF_V // D_KV)),
            pl.BlockSpec((tm, HEAD_DIM), lambda i: (i % tiles_per_seq, 0)),
            pl.BlockSpec((tm, HEAD_DIM), lambda i: (i % tiles_per_seq, 0)),
            pl.BlockSpec((1, HEAD_DIM), lambda i: (0, 0)),
            pl.BlockSpec((1, HEAD_DIM), lambda i: (0, 0)),
        ],
        out_specs=[
            pl.BlockSpec((D_ATTN, tm), lambda i: (0, i)),
            pl.BlockSpec((tm, D_KV), lambda i: (i, 0)),
            pl.BlockSpec((D_KV, tm), lambda i: (0, i)),
        ],
        out_shape=[
            jax.ShapeDtypeStruct((D_ATTN, t), BF16),
            jax.ShapeDtypeStruct((t, D_KV), BF16),
            jax.ShapeDtypeStruct((D_KV, t), BF16),
        ],
        compiler_params=pltpu.CompilerParams(
            dimension_semantics=("parallel",), vmem_limit_bytes=VMEM_LIMIT),
        name="qkv_prep",
    )(proj, proj, proj, cos, sin, gq, gk)


def _attention_kernel(qt_ref, k_ref, vt_ref, o_ref, acc_ref, *, tk, n_kv_blocks):
    tq = qt_ref.shape[1]
    acc_ref[...] = jnp.zeros_like(acc_ref)

    def body(j, carry):
        off = pl.multiple_of(j * tk, tk)
        kb = k_ref[pl.ds(off, tk), :]
        vb = vt_ref[:, pl.ds(off, tk)]
        new = []
        for g in range(GROUP):
            m, l = carry[g]
            hs = slice(g * HEAD_DIM, (g + 1) * HEAD_DIM)
            s = jnp.dot(kb, qt_ref[hs, :], preferred_element_type=F32)
            m_new = jnp.maximum(m, jnp.max(s, axis=0, keepdims=True))
            alpha = jnp.exp2(m - m_new)
            p = jnp.exp2(s - m_new)
            l_new = alpha * l + jnp.sum(p, axis=0, keepdims=True)
            pv = jnp.dot(vb, p.astype(BF16), preferred_element_type=F32)
            acc_ref[hs, :] = alpha * acc_ref[hs, :] + pv
            new.append((m_new, l_new))
        return tuple(new)

    init = tuple((jnp.full((1, tq), NEG_BIG, F32), jnp.zeros((1, tq), F32)) for _ in range(GROUP))
    stats = lax.fori_loop(0, n_kv_blocks, body, init)
    for g in range(GROUP):
        hs = slice(g * HEAD_DIM, (g + 1) * HEAD_DIM)
        out = acc_ref[hs, :] / stats[g][1]
        o_ref[:, hs] = out.T.astype(o_ref.dtype)


def _attention_unshifted_kernel(qt_ref, k_ref, vt_ref, o_ref, acc_ref, s0_ref, *, tq, tk, n_kv_blocks):
    heads = [slice(g * HEAD_DIM, (g + 1) * HEAD_DIM) for g in range(GROUP)]
    n_trips = (qt_ref.shape[1] // tq) * n_kv_blocks
    acc_ref[...] = jnp.zeros_like(acc_ref)

    def offsets(t):
        return pl.multiple_of((t // n_kv_blocks) * tq, tq), pl.multiple_of((t % n_kv_blocks) * tk, tk)

    def scores(t, g):
        qoff, koff = offsets(t)
        return jnp.dot(k_ref[pl.ds(koff, tk), :], qt_ref[heads[g], pl.ds(qoff, tq)],
                       preferred_element_type=F32)

    s0_ref[...] = scores(0, 0)

    def body(t, carry):
        qoff, koff = offsets(t)
        vb = vt_ref[:, pl.ds(koff, tk)]
        new = []
        s = s0_ref[...]
        for g in range(GROUP):
            if g + 1 < GROUP:
                s_next = scores(t, g + 1)
            else:
                s_next = scores(jnp.minimum(t + 1, n_trips - 1), 0)
            p = jnp.exp2(s)
            new.append(carry[g] + jnp.sum(p, axis=0, keepdims=True))
            acc_ref[heads[g], :] += jnp.dot(vb, p.astype(BF16), preferred_element_type=F32)
            s = s_next
        s0_ref[...] = s
        tile_done = (t % n_kv_blocks) == n_kv_blocks - 1

        @pl.when(tile_done)
        def _():
            for g in range(GROUP):
                out = acc_ref[heads[g], :] / new[g]
                o_ref[pl.ds(qoff, tq), heads[g]] = out.T.astype(o_ref.dtype)
            acc_ref[...] = jnp.zeros_like(acc_ref)

        return tuple(jnp.where(tile_done, 0.0, l) for l in new)

    init = tuple(jnp.zeros((1, tq), F32) for _ in range(GROUP))
    lax.fori_loop(0, n_trips, body, init)


def _attention(qt, k, vt, n_seq, seq_len, body):
    t = k.shape[0]
    unshifted = body is _attention_unshifted_kernel
    tq = min(256, seq_len)
    gw = GROUP * HEAD_DIM
    scratch = [pltpu.VMEM((gw, tq), F32)]
    if unshifted:
        tk = min(2048 if seq_len <= 8192 else 1024, seq_len)
        bq = min(2048, seq_len)
        scratch.append(pltpu.VMEM((tk, tq), F32))
        kern = functools.partial(body, tq=tq, tk=tk, n_kv_blocks=seq_len // tk)
    else:
        tk = min(512, seq_len)
        bq = tq
        kern = functools.partial(body, tk=tk, n_kv_blocks=seq_len // tk)
    q_blocks = seq_len // bq
    return pl.pallas_call(
        kern,
        grid=(n_seq, N_KV, q_blocks),
        in_specs=[
            pl.BlockSpec((gw, bq), lambda b, h, i: (h, b * q_blocks + i)),
            pl.BlockSpec((seq_len, HEAD_DIM), lambda b, h, i: (b, h)),
            pl.BlockSpec((HEAD_DIM, seq_len), lambda b, h, i: (h, b)),
        ],
        out_specs=pl.BlockSpec((bq, gw), lambda b, h, i: (b * q_blocks + i, h)),
        out_shape=jax.ShapeDtypeStruct((t, D_ATTN), BF16),
        scratch_shapes=scratch,
        compiler_params=pltpu.CompilerParams(
            dimension_semantics=("parallel", "parallel", "arbitrary"), vmem_limit_bytes=VMEM_LIMIT),
        name="attention_unshifted" if unshifted else "attention_online",
    )(qt, k, vt)


def _conv_kernel(a_ref, b_ref, z_ref, ap_ref, bp_ref, an_ref, bn_ref,
                 w_ref, bias_ref, lng_ref, lnb_ref, o_ref, ext_ref, sh_ref, cv_ref, *, tiles_per_seq):
    tm = a_ref.shape[0]
    pos = pl.program_id(0) % tiles_per_seq

    def glu(a, b):
        return a.astype(F32) * _sigmoid(b.astype(F32))

    ext_ref[0:HALO, :] = jnp.where(pos != 0, glu(ap_ref[...], bp_ref[...]), 0.0)
    ext_ref[HALO:HALO + tm, :] = glu(a_ref[...], b_ref[...])
    ext_ref[HALO + tm:2 * HALO + tm, :] = jnp.where(
        pos != tiles_per_seq - 1, glu(an_ref[...], bn_ref[...]), 0.0)

    span = tm + SHIFT_SPAN
    sh_ref[0] = ext_ref[0:span, :]
    for s in range(1, SUBLANES):
        sh_ref[s] = ext_ref[s:s + span, :]

    base = HALO - CONV_PAD
    for c in range(D_CONV // LANES):
        cs = slice(c * LANES, (c + 1) * LANES)
        acc = jnp.zeros((tm, LANES), F32)
        for k in range(CONV_K):
            s, a = (base + k) % SUBLANES, (base + k) // SUBLANES * SUBLANES
            acc = acc + sh_ref[s, a:a + tm, cs] * w_ref[k:k + 1, cs]
        cv_ref[:, cs] = acc + bias_ref[:, cs]

    u = cv_ref[...]
    mu = jnp.mean(u, axis=-1, keepdims=True)
    uc = u - mu
    var = jnp.mean(uc * uc, axis=-1, keepdims=True)
    y = uc * lax.rsqrt(var + EPS) * lng_ref[...] + lnb_ref[...]
    o_ref[...] = (_silu(y) * _silu(z_ref[...].astype(F32))).astype(o_ref.dtype)


def _conv_branch(proj, dw, bias, lng, lnb, seq_len):
    t = proj.shape[0]
    tm = min(128, seq_len)
    tiles_per_seq = seq_len // tm
    hb = tm // HALO
    n_hb = t // HALO
    ca, cb, cz = OFF_GLU_A // D_CONV, OFF_GLU_B // D_CONV, OFF_ZCONV // D_CONV
    prev_map = lambda c: (lambda i: (jnp.maximum(i * hb - 1, 0), c))
    next_map = lambda c: (lambda i: (jnp.minimum((i + 1) * hb, n_hb - 1), c))
    row = lambda i: (0, 0)
    return pl.pallas_call(
        functools.partial(_conv_kernel, tiles_per_seq=tiles_per_seq),
        grid=(t // tm,),
        in_specs=[
            pl.BlockSpec((tm, D_CONV), lambda i: (i, ca)),
            pl.BlockSpec((tm, D_CONV), lambda i: (i, cb)),
            pl.BlockSpec((tm, D_CONV), lambda i: (i, cz)),
            pl.BlockSpec((HALO, D_CONV), prev_map(ca)),
            pl.BlockSpec((HALO, D_CONV), prev_map(cb)),
            pl.BlockSpec((HALO, D_CONV), next_map(ca)),
            pl.BlockSpec((HALO, D_CONV), next_map(cb)),
            pl.BlockSpec((CONV_K, D_CONV), row),
            pl.BlockSpec((1, D_CONV), row),
            pl.BlockSpec((1, D_CONV), row),
            pl.BlockSpec((1, D_CONV), row),
        ],
        out_specs=pl.BlockSpec((tm, D_CONV), lambda i: (i, 0)),
        out_shape=jax.ShapeDtypeStruct((t, D_CONV), BF16),
        scratch_shapes=[pltpu.VMEM((tm + 2 * HALO, D_CONV), F32),
                        pltpu.VMEM((SUBLANES, tm + SHIFT_SPAN, D_CONV), F32),
                        pltpu.VMEM((tm, D_CONV), F32)],
        compiler_params=pltpu.CompilerParams(
            dimension_semantics=("parallel",), vmem_limit_bytes=VMEM_LIMIT),
        name="conv_branch",
    )(proj, proj, proj, proj, proj, proj, proj, dw, bias, lng, lnb)


def _merge_kernel(u_ref, a_ref, z_ref, gc_ref, ga_ref, x_ref, wc_ref, wa_ref, wo_ref, pg_ref, o_ref):
    y_conv = jnp.dot(u_ref[...], wc_ref[...], preferred_element_type=F32)
    ag = (a_ref[...].astype(F32) * _silu(z_ref[...].astype(F32))).astype(BF16)
    y_attn = jnp.dot(ag, wa_ref[...], preferred_element_type=F32)
    m = _sigmoid(gc_ref[...].astype(F32)) * y_conv + _sigmoid(ga_ref[...].astype(F32)) * y_attn
    out = jnp.dot(m.astype(BF16), wo_ref[...], preferred_element_type=F32)
    ms = jnp.mean(out * out, axis=-1, keepdims=True)
    o_ref[...] = x_ref[...] + out * lax.rsqrt(ms + EPS) * pg_ref[...]


def _merge_out(u, a, proj, x, wc, wa, wo, pg):
    t = x.shape[0]
    tm = min(256, t)
    const = lambda i: (0, 0)
    resident = functools.partial(pl.BlockSpec, index_map=const, pipeline_mode=pl.Buffered(1))
    return pl.pallas_call(
        _merge_kernel,
        grid=(t // tm,),
        in_specs=[
            pl.BlockSpec((tm, D_CONV), lambda i: (i, 0)),
            pl.BlockSpec((tm, D_ATTN), lambda i: (i, 0)),
            pl.BlockSpec((tm, D_ATTN), lambda i: (i, OFF_ZATTN // D_ATTN)),
            pl.BlockSpec((tm, D_MODEL), lambda i: (i, OFF_GCONV // D_MODEL)),
            pl.BlockSpec((tm, D_MODEL), lambda i: (i, OFF_GATTN // D_MODEL)),
            pl.BlockSpec((tm, D_MODEL), lambda i: (i, 0)),
            resident((D_CONV, D_MODEL)),
            resident((D_ATTN, D_MODEL)),
            resident((D_MODEL, D_MODEL)),
            pl.BlockSpec((1, D_MODEL), const),
        ],
        out_specs=pl.BlockSpec((tm, D_MODEL), lambda i: (i, 0)),
        out_shape=jax.ShapeDtypeStruct((t, D_MODEL), F32),
        compiler_params=pltpu.CompilerParams(
            dimension_semantics=("parallel",), vmem_limit_bytes=VMEM_LIMIT),
        name="merge_out",
    )(u, a, proj, proj, proj, x, wc, wa, wo, pg)


def _pair_major(w):
    lead = w.shape[:-1]
    n = w.shape[-1] // HEAD_DIM
    w = w.reshape(lead + (n, 2, 2, ROPE_FREQS))
    w = jnp.swapaxes(w, -2, -3)
    return w.reshape(lead + (n * HEAD_DIM,))


def _pack_w_in(w_in):
    o = [0, D_CONV, 2 * D_CONV, 3 * D_CONV]
    o.append(o[-1] + D_ATTN)
    o.append(o[-1] + D_KV)
    o.append(o[-1] + D_KV)
    o.append(o[-1] + D_ATTN)
    o.append(o[-1] + D_MODEL)
    o.append(o[-1] + D_MODEL)
    glu_a, glu_b, z_conv, q, k, v, z_attn, g_conv, g_attn = (
        w_in[..., o[n]:o[n + 1]] for n in range(9))
    packed = jnp.concatenate(
        [_pair_major(q), z_attn, g_conv, g_attn, glu_a, glu_b, z_conv, _pair_major(k), v], axis=-1)
    return packed.astype(BF16)


def _rope_tables(seq_len):
    rows = seq_len // GRID_W
    row_idx = jnp.broadcast_to(jnp.arange(rows, dtype=F32)[:, None], (rows, GRID_W)).reshape(-1)
    col_idx = jnp.broadcast_to(jnp.arange(GRID_W, dtype=F32)[None, :], (rows, GRID_W)).reshape(-1)
    inv_freq = ROPE_THETA ** (-(jnp.arange(ROPE_FREQS, dtype=F32) * 2.0) / ROPE_AXIS_DIM)
    ar = row_idx[:, None] * inv_freq
    ac = col_idx[:, None] * inv_freq
    cos = jnp.concatenate([jnp.cos(ar), jnp.cos(ac), jnp.cos(ar), jnp.cos(ac)], axis=-1)
    sin = jnp.concatenate([-jnp.sin(ar), -jnp.sin(ac), jnp.sin(ar), jnp.sin(ac)], axis=-1)
    return cos, sin


def _layer(x, n_seq, seq_len, tables, p):
    cos, sin = tables
    proj = _inproj(x, p["pre_g"], p["w_in"])
    qt, k, vt = _qkv_prep(proj, cos, sin, p["gq"], p["gk"], seq_len)
    score_bound = HEAD_DIM * Q_SCALE * jnp.max(jnp.abs(p["gq"])) * jnp.max(jnp.abs(p["gk"]))
    a = lax.cond(
        score_bound <= MAX_UNSHIFTED,
        functools.partial(_attention, n_seq=n_seq, seq_len=seq_len, body=_attention_unshifted_kernel),
        functools.partial(_attention, n_seq=n_seq, seq_len=seq_len, body=_attention_kernel),
        qt, k, vt)
    u = _conv_branch(proj, p["dw"], p["dw_b"], p["ln_g"], p["ln_b"], seq_len)
    return _merge_out(u, a, proj, x, p["w_conv_out"], p["w_attn_o"], p["w_out"], p["post_g"])


def kernel(x_prompt, x_sample, pre_norm_g, w_in, dw_kernel, dw_bias, conv_ln_g, conv_ln_b,
           w_conv_out, q_norm_g, k_norm_g, w_attn_o, w_out, post_norm_g):
    depth = w_in.shape[0]
    w_in_p = _pack_w_in(w_in)
    wc = w_conv_out.astype(BF16)
    wa = w_attn_o.astype(BF16)
    wo = w_out.astype(BF16)
    gq = _pair_major(q_norm_g)
    gk = _pair_major(k_norm_g)

    streams = []
    for x in (x_prompt, x_sample):
        b, s, d = x.shape
        streams.append((x.reshape(b * s, d), b, s, _rope_tables(s)))

    outs = []
    for x2, b, s, tables in streams:
        y = x2
        for l in range(depth):
            p = dict(
                pre_g=pre_norm_g[l][None, :], w_in=w_in_p[l], dw=dw_kernel[l, :, 0, :],
                dw_b=dw_bias[l][None, :], ln_g=conv_ln_g[l][None, :], ln_b=conv_ln_b[l][None, :],
                w_conv_out=wc[l], gq=gq[l][None, :], gk=gk[l][None, :], w_attn_o=wa[l],
                w_out=wo[l], post_g=post_norm_g[l][None, :])
            y = _layer(y, b, s, tables, p)
        outs.append(y.reshape(b, s, d))
    return tuple(outs)
```

```python
import functools
import math

import jax
import jax.numpy as jnp
from jax import lax
from jax.experimental import pallas as pl
from jax.experimental.pallas import tpu as pltpu

F32 = jnp.float32
BF16 = jnp.bfloat16

D_MODEL = 2048
DEPTH = 4
GRID_W = 64
D_CONV = D_MODEL // 2
CONV_K = 31
CONV_PAD = CONV_K // 2
HEAD_DIM = 128
N_HEADS = D_MODEL // HEAD_DIM
N_KV = 4
GROUP = N_HEADS // N_KV
D_ATTN = N_HEADS * HEAD_DIM
D_KV = N_KV * HEAD_DIM
ROPE_AXIS_DIM = HEAD_DIM // 2
ROPE_FREQS = ROPE_AXIS_DIM // 2
ROPE_THETA = 10000.0
EPS = 1e-6
N_IN = 3 * D_CONV + 2 * D_ATTN + 2 * D_KV + 2 * D_MODEL

OFF_Q = 0
OFF_ZATTN = OFF_Q + D_ATTN
OFF_GCONV = OFF_ZATTN + D_ATTN
OFF_GATTN = OFF_GCONV + D_MODEL
OFF_GLU_A = OFF_GATTN + D_MODEL
OFF_GLU_B = OFF_GLU_A + D_CONV
OFF_ZCONV = OFF_GLU_B + D_CONV
OFF_K = OFF_ZCONV + D_CONV
OFF_V = OFF_K + D_KV

HALO = 16
LANES = 128
SUBLANES = 8
CONV_ACC_ROWS = 128
SHIFT_SPAN = (HALO - CONV_PAD + CONV_K - 1) // SUBLANES * SUBLANES
VMEM_LIMIT = 56 * 1024 * 1024
NEG_BIG = -1e30
MAX_UNSHIFTED = 60.0
Q_SCALE = (1.0 / math.sqrt(HEAD_DIM)) * math.log2(math.e)


def _sigmoid(x):
    return 1.0 / (1.0 + jnp.exp(-x))


def _silu(x):
    return x * _sigmoid(x)


def _inproj_kernel(x_ref, g_ref, w_ref, o_ref, h_ref):
    @pl.when(pl.program_id(1) == 0)
    def _():
        x = x_ref[...]
        ms = jnp.mean(x * x, axis=-1, keepdims=True)
        h_ref[...] = (x * lax.rsqrt(ms + EPS) * g_ref[...]).astype(BF16)

    o_ref[...] = jnp.dot(h_ref[...], w_ref[...], preferred_element_type=F32).astype(o_ref.dtype)


def _inproj(x, g, w):
    t = x.shape[0]
    tm = min(1024, t)
    tn = 1024
    return pl.pallas_call(
        _inproj_kernel,
        grid=(t // tm, N_IN // tn),
        in_specs=[
            pl.BlockSpec((tm, D_MODEL), lambda i, j: (i, 0)),
            pl.BlockSpec((1, D_MODEL), lambda i, j: (0, 0)),
            pl.BlockSpec((D_MODEL, tn), lambda i, j: (0, j)),
        ],
        out_specs=pl.BlockSpec((tm, tn), lambda i, j: (i, j)),
        out_shape=jax.ShapeDtypeStruct((t, N_IN), BF16),
        scratch_shapes=[pltpu.VMEM((tm, D_MODEL), BF16)],
        compiler_params=pltpu.CompilerParams(
            dimension_semantics=("parallel", "arbitrary"), vmem_limit_bytes=VMEM_LIMIT),
        name="inproj",
    )(x, g, w)


def _norm_rope(x, g, cos, sin):
    ms = jnp.mean(x * x, axis=-1, keepdims=True)
    y = x * lax.rsqrt(ms + EPS) * g
    return y * cos + pltpu.roll(y, HEAD_DIM // 2, axis=1) * sin


def _qkv_prep_kernel(q_ref, k_ref, v_ref, cos_ref, sin_ref, gq_ref, gk_ref,
                     qt_ref, ko_ref, vt_ref):
    cos = cos_ref[...]
    sin = sin_ref[...]
    gq = gq_ref[...] * Q_SCALE
    gk = gk_ref[...]
    for h in range(N_HEADS):
        hs = slice(h * HEAD_DIM, (h + 1) * HEAD_DIM)
        y = _norm_rope(q_ref[:, hs].astype(F32), gq, cos, sin)
        qt_ref[hs, :] = y.T.astype(BF16)
    for h in range(N_KV):
        hs = slice(h * HEAD_DIM, (h + 1) * HEAD_DIM)
        y = _norm_rope(k_ref[:, hs].astype(F32), gk, cos, sin)
        ko_ref[:, hs] = y.astype(BF16)
        vt_ref[hs, :] = v_ref[:, hs].astype(F32).T.astype(BF16)


def _qkv_prep(proj, cos, sin, gq, gk, seq_len):
    t = proj.shape[0]
    tm = min(512, seq_len)
    tiles_per_seq = seq_len // tm
    return pl.pallas_call(
        _qkv_prep_kernel,
        grid=(t // tm,),
        in_specs=[
            pl.BlockSpec((tm, D_ATTN), lambda i: (i, OFF_Q // D_ATTN)),
            pl.BlockSpec((tm, D_KV), lambda i: (i, OFF_K // D_KV)),
            pl.BlockSpec((tm, D_KV), lambda i: (i, OFF_V // D_KV)),
            pl.BlockSpec((tm, HEAD_DIM), lambda i: (i % tiles_per_seq, 0)),
            pl.BlockSpec((tm, HEAD_DIM), lambda i: (i % tiles_per_seq, 0)),
            pl.BlockSpec((1, HEAD_DIM), lambda i: (0, 0)),
            pl.BlockSpec((1, HEAD_DIM), lambda i: (0, 0)),
        ],
        out_specs=[
            pl.BlockSpec((D_ATTN, tm), lambda i: (0, i)),
            pl.BlockSpec((tm, D_KV), lambda i: (i, 0)),
            pl.BlockSpec((D_KV, tm), lambda i: (0, i)),
        ],
        out_shape=[
            jax.ShapeDtypeStruct((D_ATTN, t), BF16),
            jax.ShapeDtypeStruct((t, D_KV), BF16),
            jax.ShapeDtypeStruct((D_KV, t), BF16),
        ],
        compiler_params=pltpu.CompilerParams(
            dimension_semantics=("parallel",), vmem_limit_bytes=VMEM_LIMIT),
        name="qkv_prep",
    )(proj, proj, proj, cos, sin, gq, gk)


def _attention_kernel(qt_ref, k_ref, vt_ref, o_ref, acc_ref, *, tk, n_kv_blocks):
    tq = qt_ref.shape[1]
    acc_ref[...] = jnp.zeros_like(acc_ref)

    def body(j, carry):
        off = pl.multiple_of(j * tk, tk)
        kb = k_ref[pl.ds(off, tk), :]
        vb = vt_ref[:, pl.ds(off, tk)]
        new = []
        for g in range(GROUP):
            m, l = carry[g]
            hs = slice(g * HEAD_DIM, (g + 1) * HEAD_DIM)
            s = jnp.dot(kb, qt_ref[hs, :], preferred_element_type=F32)
            m_new = jnp.maximum(m, jnp.max(s, axis=0, keepdims=True))
            alpha = jnp.exp2(m - m_new)
            p = jnp.exp2(s - m_new)
            l_new = alpha * l + jnp.sum(p, axis=0, keepdims=True)
            pv = jnp.dot(vb, p.astype(BF16), preferred_element_type=F32)
            acc_ref[hs, :] = alpha * acc_ref[hs, :] + pv
            new.append((m_new, l_new))
        return tuple(new)

    init = tuple((jnp.full((1, tq), NEG_BIG, F32), jnp.zeros((1, tq), F32)) for _ in range(GROUP))
    stats = lax.fori_loop(0, n_kv_blocks, body, init)
    for g in range(GROUP):
        hs = slice(g * HEAD_DIM, (g + 1) * HEAD_DIM)
        out = acc_ref[hs, :] / stats[g][1]
        o_ref[:, hs] = out.T.astype(o_ref.dtype)


def _attention_unshifted_kernel(qt_ref, k_ref, vt_ref, o_ref, acc_ref, s0_ref, *, tq, tk, n_kv_blocks):
    heads = [slice(g * HEAD_DIM, (g + 1) * HEAD_DIM) for g in range(GROUP)]
    n_trips = (qt_ref.shape[1] // tq) * n_kv_blocks
    acc_ref[...] = jnp.zeros_like(acc_ref)

    def offsets(t):
        return pl.multiple_of((t // n_kv_blocks) * tq, tq), pl.multiple_of((t % n_kv_blocks) * tk, tk)

    def scores(t, g):
        qoff, koff = offsets(t)
        return jnp.dot(k_ref[pl.ds(koff, tk), :], qt_ref[heads[g], pl.ds(qoff, tq)],
                       preferred_element_type=F32)

    s0_ref[...] = scores(0, 0)

    def body(t, carry):
        qoff, koff = offsets(t)
        vb = vt_ref[:, pl.ds(koff, tk)]
        new = []
        s = s0_ref[...]
        for g in range(GROUP):
            if g + 1 < GROUP:
                s_next = scores(t, g + 1)
            else:
                s_next = scores(jnp.minimum(t + 1, n_trips - 1), 0)
            p = jnp.exp2(s)
            new.append(carry[g] + jnp.sum(p, axis=0, keepdims=True))
            acc_ref[heads[g], :] += jnp.dot(vb, p.astype(BF16), preferred_element_type=F32)
            s = s_next
        s0_ref[...] = s
        tile_done = (t % n_kv_blocks) == n_kv_blocks - 1

        @pl.when(tile_done)
        def _():
            for g in range(GROUP):
                out = acc_ref[heads[g], :] / new[g]
                o_ref[pl.ds(qoff, tq), heads[g]] = out.T.astype(o_ref.dtype)
            acc_ref[...] = jnp.zeros_like(acc_ref)

        return tuple(jnp.where(tile_done, 0.0, l) for l in new)

    init = tuple(jnp.zeros((1, tq), F32) for _ in range(GROUP))
    lax.fori_loop(0, n_trips, body, init)


def _attention(qt, k, vt, n_seq, seq_len, body):
    t = k.shape[0]
    unshifted = body is _attention_unshifted_kernel
    tq = min(256, seq_len)
    gw = GROUP * HEAD_DIM
    scratch = [pltpu.VMEM((gw, tq), F32)]
    if unshifted:
        tk = min(4096 if seq_len <= 8192 else 2048, seq_len)
        bq = min(2048, seq_len)
        scratch.append(pltpu.VMEM((tk, tq), F32))
        kern = functools.partial(body, tq=tq, tk=tk, n_kv_blocks=seq_len // tk)
    else:
        tk = min(512, seq_len)
        bq = tq
        kern = functools.partial(body, tk=tk, n_kv_blocks=seq_len // tk)
    q_blocks = seq_len // bq
    return pl.pallas_call(
        kern,
        grid=(n_seq, N_KV, q_blocks),
        in_specs=[
            pl.BlockSpec((gw, bq), lambda b, h, i: (h, b * q_blocks + i)),
            pl.BlockSpec((seq_len, HEAD_DIM), lambda b, h, i: (b, h)),
            pl.BlockSpec((HEAD_DIM, seq_len), lambda b, h, i: (h, b)),
        ],
        out_specs=pl.BlockSpec((bq, gw), lambda b, h, i: (b * q_blocks + i, h)),
        out_shape=jax.ShapeDtypeStruct((t, D_ATTN), BF16),
        scratch_shapes=scratch,
        compiler_params=pltpu.CompilerParams(
            dimension_semantics=("parallel", "parallel", "arbitrary"), vmem_limit_bytes=VMEM_LIMIT),
        name="attention_unshifted" if unshifted else "attention_online",
    )(qt, k, vt)


def _conv_kernel(a_ref, b_ref, z_ref, ap_ref, bp_ref, an_ref, bn_ref,
                 w_ref, bias_ref, lng_ref, lnb_ref, o_ref, ext_ref, sh_ref, cv_ref, *, tiles_per_seq):
    tm = a_ref.shape[0]
    pos = pl.program_id(0) % tiles_per_seq

    def glu(a, b):
        return a.astype(F32) * _sigmoid(b.astype(F32))

    ext_ref[0:HALO, :] = jnp.where(pos != 0, glu(ap_ref[...], bp_ref[...]), 0.0)
    ext_ref[HALO:HALO + tm, :] = glu(a_ref[...], b_ref[...])
    ext_ref[HALO + tm:2 * HALO + tm, :] = jnp.where(
        pos != tiles_per_seq - 1, glu(an_ref[...], bn_ref[...]), 0.0)

    span = tm + SHIFT_SPAN
    sh_ref[0] = ext_ref[0:span, :]
    for s in range(1, SUBLANES):
        sh_ref[s] = ext_ref[s:s + span, :]

    base = HALO - CONV_PAD
    rows = min(tm, CONV_ACC_ROWS)
    for r0 in range(0, tm, rows):
        for c in range(D_CONV // LANES):
            cs = slice(c * LANES, (c + 1) * LANES)
            acc = jnp.zeros((rows, LANES), F32)
            for k in range(CONV_K):
                s, a = (base + k) % SUBLANES, (base + k) // SUBLANES * SUBLANES
                acc = acc + sh_ref[s, r0 + a:r0 + a + rows, cs] * w_ref[k:k + 1, cs]
            cv_ref[r0:r0 + rows, cs] = acc + bias_ref[:, cs]

    u = cv_ref[...]
    mu = jnp.mean(u, axis=-1, keepdims=True)
    uc = u - mu
    var = jnp.mean(uc * uc, axis=-1, keepdims=True)
    y = uc * lax.rsqrt(var + EPS) * lng_ref[...] + lnb_ref[...]
    o_ref[...] = (_silu(y) * _silu(z_ref[...].astype(F32))).astype(o_ref.dtype)


def _conv_branch(proj, dw, bias, lng, lnb, seq_len):
    t = proj.shape[0]
    tm = min(256, seq_len)
    tiles_per_seq = seq_len // tm
    hb = tm // HALO
    n_hb = t // HALO
    ca, cb, cz = OFF_GLU_A // D_CONV, OFF_GLU_B // D_CONV, OFF_ZCONV // D_CONV
    prev_map = lambda c: (lambda i: (jnp.maximum(i * hb - 1, 0), c))
    next_map = lambda c: (lambda i: (jnp.minimum((i + 1) * hb, n_hb - 1), c))
    row = lambda i: (0, 0)
    return pl.pallas_call(
        functools.partial(_conv_kernel, tiles_per_seq=tiles_per_seq),
        grid=(t // tm,),
        in_specs=[
            pl.BlockSpec((tm, D_CONV), lambda i: (i, ca)),
            pl.BlockSpec((tm, D_CONV), lambda i: (i, cb)),
            pl.BlockSpec((tm, D_CONV), lambda i: (i, cz)),
            pl.BlockSpec((HALO, D_CONV), prev_map(ca)),
            pl.BlockSpec((HALO, D_CONV), prev_map(cb)),
            pl.BlockSpec((HALO, D_CONV), next_map(ca)),
            pl.BlockSpec((HALO, D_CONV), next_map(cb)),
            pl.BlockSpec((CONV_K, D_CONV), row),
            pl.BlockSpec((1, D_CONV), row),
            pl.BlockSpec((1, D_CONV), row),
            pl.BlockSpec((1, D_CONV), row),
        ],
        out_specs=pl.BlockSpec((tm, D_CONV), lambda i: (i, 0)),
        out_shape=jax.ShapeDtypeStruct((t, D_CONV), BF16),
        scratch_shapes=[pltpu.VMEM((tm + 2 * HALO, D_CONV), F32),
                        pltpu.VMEM((SUBLANES, tm + SHIFT_SPAN, D_CONV), F32),
                        pltpu.VMEM((tm, D_CONV), F32)],
        compiler_params=pltpu.CompilerParams(
            dimension_semantics=("parallel",), vmem_limit_bytes=VMEM_LIMIT),
        name="conv_branch",
    )(proj, proj, proj, proj, proj, proj, proj, dw, bias, lng, lnb)


def _merge_kernel(u_ref, a_ref, z_ref, gc_ref, ga_ref, x_ref, wc_ref, wa_ref, wo_ref, pg_ref, o_ref):
    y_conv = jnp.dot(u_ref[...], wc_ref[...], preferred_element_type=F32)
    ag = (a_ref[...].astype(F32) * _silu(z_ref[...].astype(F32))).astype(BF16)
    y_attn = jnp.dot(ag, wa_ref[...], preferred_element_type=F32)
    m = _sigmoid(gc_ref[...].astype(F32)) * y_conv + _sigmoid(ga_ref[...].astype(F32)) * y_attn
    out = jnp.dot(m.astype(BF16), wo_ref[...], preferred_element_type=F32)
    ms = jnp.mean(out * out, axis=-1, keepdims=True)
    o_ref[...] = x_ref[...] + out * lax.rsqrt(ms + EPS) * pg_ref[...]


def _merge_out(u, a, proj, x, wc, wa, wo, pg):
    t = x.shape[0]
    tm = min(256, t)
    const = lambda i: (0, 0)
    resident = functools.partial(pl.BlockSpec, index_map=const, pipeline_mode=pl.Buffered(1))
    return pl.pallas_call(
        _merge_kernel,
        grid=(t // tm,),
        in_specs=[
            pl.BlockSpec((tm, D_CONV), lambda i: (i, 0)),
            pl.BlockSpec((tm, D_ATTN), lambda i: (i, 0)),
            pl.BlockSpec((tm, D_ATTN), lambda i: (i, OFF_ZATTN // D_ATTN)),
            pl.BlockSpec((tm, D_MODEL), lambda i: (i, OFF_GCONV // D_MODEL)),
            pl.BlockSpec((tm, D_MODEL), lambda i: (i, OFF_GATTN // D_MODEL)),
            pl.BlockSpec((tm, D_MODEL), lambda i: (i, 0)),
            resident((D_CONV, D_MODEL)),
            resident((D_ATTN, D_MODEL)),
            resident((D_MODEL, D_MODEL)),
            pl.BlockSpec((1, D_MODEL), const),
        ],
        out_specs=pl.BlockSpec((tm, D_MODEL), lambda i: (i, 0)),
        out_shape=jax.ShapeDtypeStruct((t, D_MODEL), F32),
        compiler_params=pltpu.CompilerParams(
            dimension_semantics=("parallel",), vmem_limit_bytes=VMEM_LIMIT),
        name="merge_out",
    )(u, a, proj, proj, proj, x, wc, wa, wo, pg)


def _pair_major(w):
    lead = w.shape[:-1]
    n = w.shape[-1] // HEAD_DIM
    w = w.reshape(lead + (n, 2, 2, ROPE_FREQS))
    w = jnp.swapaxes(w, -2, -3)
    return w.reshape(lead + (n * HEAD_DIM,))


def _pack_w_in(w_in):
    o = [0, D_CONV, 2 * D_CONV, 3 * D_CONV]
    o.append(o[-1] + D_ATTN)
    o.append(o[-1] + D_KV)
    o.append(o[-1] + D_KV)
    o.append(o[-1] + D_ATTN)
    o.append(o[-1] + D_MODEL)
    o.append(o[-1] + D_MODEL)
    glu_a, glu_b, z_conv, q, k, v, z_attn, g_conv, g_attn = (
        w_in[..., o[n]:o[n + 1]] for n in range(9))
    packed = jnp.concatenate(
        [_pair_major(q), z_attn, g_conv, g_attn, glu_a, glu_b, z_conv, _pair_major(k), v], axis=-1)
    return packed.astype(BF16)


def _rope_tables(seq_len):
    rows = seq_len // GRID_W
    row_idx = jnp.broadcast_to(jnp.arange(rows, dtype=F32)[:, None], (rows, GRID_W)).reshape(-1)
    col_idx = jnp.broadcast_to(jnp.arange(GRID_W, dtype=F32)[None, :], (rows, GRID_W)).reshape(-1)
    inv_freq = ROPE_THETA ** (-(jnp.arange(ROPE_FREQS, dtype=F32) * 2.0) / ROPE_AXIS_DIM)
    ar = row_idx[:, None] * inv_freq
    ac = col_idx[:, None] * inv_freq
    cos = jnp.concatenate([jnp.cos(ar), jnp.cos(ac), jnp.cos(ar), jnp.cos(ac)], axis=-1)
    sin = jnp.concatenate([-jnp.sin(ar), -jnp.sin(ac), jnp.sin(ar), jnp.sin(ac)], axis=-1)
    return cos, sin


def _layer(x, n_seq, seq_len, tables, p):
    cos, sin = tables
    proj = _inproj(x, p["pre_g"], p["w_in"])
    qt, k, vt = _qkv_prep(proj, cos, sin, p["gq"], p["gk"], seq_len)
    score_bound = HEAD_DIM * Q_SCALE * jnp.max(jnp.abs(p["gq"])) * jnp.max(jnp.abs(p["gk"]))
    a = lax.cond(
        score_bound <= MAX_UNSHIFTED,
        functools.partial(_attention, n_seq=n_seq, seq_len=seq_len, body=_attention_unshifted_kernel),
        functools.partial(_attention, n_seq=n_seq, seq_len=seq_len, body=_attention_kernel),
        qt, k, vt)
    u = _conv_branch(proj, p["dw"], p["dw_b"], p["ln_g"], p["ln_b"], seq_len)
    return _merge_out(u, a, proj, x, p["w_conv_out"], p["w_attn_o"], p["w_out"], p["post_g"])


def kernel(x_prompt, x_sample, pre_norm_g, w_in, dw_kernel, dw_bias, conv_ln_g, conv_ln_b,
           w_conv_out, q_norm_g, k_norm_g, w_attn_o, w_out, post_norm_g):
    depth = w_in.shape[0]
    w_in_p = _pack_w_in(w_in)
    wc = w_conv_out.astype(BF16)
    wa = w_attn_o.astype(BF16)
    wo = w_out.astype(BF16)
    gq = _pair_major(q_norm_g)
    gk = _pair_major(k_norm_g)

    streams = []
    for x in (x_prompt, x_sample):
        b, s, d = x.shape
        streams.append((x.reshape(b * s, d), b, s, _rope_tables(s)))

    outs = []
    for x2, b, s, tables in streams:
        y = x2
        for l in range(depth):
            p = dict(
                pre_g=pre_norm_g[l][None, :], w_in=w_in_p[l], dw=dw_kernel[l, :, 0, :],
                dw_b=dw_bias[l][None, :], ln_g=conv_ln_g[l][None, :], ln_b=conv_ln_b[l][None, :],
                w_conv_out=wc[l], gq=gq[l][None, :], gk=gk[l][None, :], w_attn_o=wa[l],
                w_out=wo[l], post_g=post_norm_g[l][None, :])
            y = _layer(y, b, s, tables, p)
        outs.append(y.reshape(b, s, d))
    return tuple(outs)
```

```python
import functools
import math

import jax
import jax.numpy as jnp
from jax import lax
from jax.experimental import pallas as pl
from jax.experimental.pallas import tpu as pltpu

F32 = jnp.float32
BF16 = jnp.bfloat16

D_MODEL = 2048
DEPTH = 4
GRID_W = 64
D_CONV = D_MODEL // 2
CONV_K = 31
CONV_PAD = CONV_K // 2
HEAD_DIM = 128
N_HEADS = D_MODEL // HEAD_DIM
N_KV = 4
GROUP = N_HEADS // N_KV
D_ATTN = N_HEADS * HEAD_DIM
D_KV = N_KV * HEAD_DIM
ROPE_AXIS_DIM = HEAD_DIM // 2
ROPE_FREQS = ROPE_AXIS_DIM // 2
ROPE_THETA = 10000.0
EPS = 1e-6
N_IN = 3 * D_CONV + 2 * D_ATTN + 2 * D_KV + 2 * D_MODEL

OFF_Q = 0
OFF_ZATTN = OFF_Q + D_ATTN
OFF_GCONV = OFF_ZATTN + D_ATTN
OFF_GATTN = OFF_GCONV + D_MODEL
OFF_GLU_A = OFF_GATTN + D_MODEL
OFF_GLU_B = OFF_GLU_A + D_CONV
OFF_ZCONV = OFF_GLU_B + D_CONV
OFF_K = OFF_ZCONV + D_CONV
OFF_V = OFF_K + D_KV

HALO = 16
LANES = 128
SUBLANES = 8
CONV_ACC_ROWS = 128
SHIFT_SPAN = (HALO - CONV_PAD + CONV_K - 1) // SUBLANES * SUBLANES
VMEM_LIMIT = 56 * 1024 * 1024
NEG_BIG = -1e30
MAX_UNSHIFTED = 60.0
Q_SCALE = (1.0 / math.sqrt(HEAD_DIM)) * math.log2(math.e)


def _sigmoid(x):
    return 1.0 / (1.0 + jnp.exp(-x))


def _silu(x):
    return x * _sigmoid(x)


def _inproj_kernel(x_ref, g_ref, w_ref, o_ref, h_ref):
    @pl.when(pl.program_id(1) == 0)
    def _():
        x = x_ref[...]
        ms = jnp.mean(x * x, axis=-1, keepdims=True)
        h_ref[...] = (x * lax.rsqrt(ms + EPS) * g_ref[...]).astype(BF16)

    o_ref[...] = jnp.dot(h_ref[...], w_ref[...], preferred_element_type=F32).astype(o_ref.dtype)


def _inproj(x, g, w):
    t = x.shape[0]
    tm = min(1024, t)
    tn = 1024
    return pl.pallas_call(
        _inproj_kernel,
        grid=(t // tm, N_IN // tn),
        in_specs=[
            pl.BlockSpec((tm, D_MODEL), lambda i, j: (i, 0)),
            pl.BlockSpec((1, D_MODEL), lambda i, j: (0, 0)),
            pl.BlockSpec((D_MODEL, tn), lambda i, j: (0, j)),
        ],
        out_specs=pl.BlockSpec((tm, tn), lambda i, j: (i, j)),
        out_shape=jax.ShapeDtypeStruct((t, N_IN), BF16),
        scratch_shapes=[pltpu.VMEM((tm, D_MODEL), BF16)],
        compiler_params=pltpu.CompilerParams(
            dimension_semantics=("parallel", "arbitrary"), vmem_limit_bytes=VMEM_LIMIT),
        name="inproj",
    )(x, g, w)


def _norm_rope(x, g, cos, sin):
    ms = jnp.mean(x * x, axis=-1, keepdims=True)
    y = x * lax.rsqrt(ms + EPS) * g
    return y * cos + pltpu.roll(y, HEAD_DIM // 2, axis=1) * sin


def _qkv_prep_kernel(q_ref, k_ref, v_ref, cos_ref, sin_ref, gq_ref, gk_ref,
                     qt_ref, ko_ref, vt_ref):
    cos = cos_ref[...]
    sin = sin_ref[...]
    gq = gq_ref[...] * Q_SCALE
    gk = gk_ref[...]
    for h in range(N_HEADS):
        hs = slice(h * HEAD_DIM, (h + 1) * HEAD_DIM)
        y = _norm_rope(q_ref[:, hs].astype(F32), gq, cos, sin)
        qt_ref[hs, :] = y.T.astype(BF16)
    for h in range(N_KV):
        hs = slice(h * HEAD_DIM, (h + 1) * HEAD_DIM)
        y = _norm_rope(k_ref[:, hs].astype(F32), gk, cos, sin)
        ko_ref[:, hs] = y.astype(BF16)
        vt_ref[hs, :] = v_ref[:, hs].astype(F32).T.astype(BF16)


def _qkv_prep(proj, cos, sin, gq, gk, seq_len):
    t = proj.shape[0]
    tm = min(512, seq_len)
    tiles_per_seq = seq_len // tm
    return pl.pallas_call(
        _qkv_prep_kernel,
        grid=(t // tm,),
        in_specs=[
            pl.BlockSpec((tm, D_ATTN), lambda i: (i, OFF_Q // D_ATTN)),
            pl.BlockSpec((tm, D_KV), lambda i: (i, OFF_K // D_KV)),
            pl.BlockSpec((tm, D_KV), lambda i: (i, OFF_V // D_KV)),
            pl.BlockSpec((tm, HEAD_DIM), lambda i: (i % tiles_per_seq, 0)),
            pl.BlockSpec((tm, HEAD_DIM), lambda i: (i % tiles_per_seq, 0)),
            pl.BlockSpec((1, HEAD_DIM), lambda i: (0, 0)),
            pl.BlockSpec((1, HEAD_DIM), lambda i: (0, 0)),
        ],
        out_specs=[
            pl.BlockSpec((D_ATTN, tm), lambda i: (0, i)),
            pl.BlockSpec((tm, D_KV), lambda i: (i, 0)),
            pl.BlockSpec((D_KV, tm), lambda i: (0, i)),
        ],
        out_shape=[
            jax.ShapeDtypeStruct((D_ATTN, t), BF16),
            jax.ShapeDtypeStruct((t, D_KV), BF16),
            jax.ShapeDtypeStruct((D_KV, t), BF16),
        ],
        compiler_params=pltpu.CompilerParams(
            dimension_semantics=("parallel",), vmem_limit_bytes=VMEM_LIMIT),
        name="qkv_prep",
    )(proj, proj, proj, cos, sin, gq, gk)


def _attention_kernel(qt_ref, k_ref, vt_ref, o_ref, acc_ref, *, tk, n_kv_blocks):
    tq = qt_ref.shape[1]
    acc_ref[...] = jnp.zeros_like(acc_ref)

    def body(j, carry):
        off = pl.multiple_of(j * tk, tk)
        kb = k_ref[pl.ds(off, tk), :]
        vb = vt_ref[:, pl.ds(off, tk)]
        new = []
        for g in range(GROUP):
            m, l = carry[g]
            hs = slice(g * HEAD_DIM, (g + 1) * HEAD_DIM)
            s = jnp.dot(kb, qt_ref[hs, :], preferred_element_type=F32)
            m_new = jnp.maximum(m, jnp.max(s, axis=0, keepdims=True))
            alpha = jnp.exp2(m - m_new)
            p = jnp.exp2(s - m_new)
            l_new = alpha * l + jnp.sum(p, axis=0, keepdims=True)
            pv = jnp.dot(vb, p.astype(BF16), preferred_element_type=F32)
            acc_ref[hs, :] = alpha * acc_ref[hs, :] + pv
            new.append((m_new, l_new))
        return tuple(new)

    init = tuple((jnp.full((1, tq), NEG_BIG, F32), jnp.zeros((1, tq), F32)) for _ in range(GROUP))
    stats = lax.fori_loop(0, n_kv_blocks, body, init)
    for g in range(GROUP):
        hs = slice(g * HEAD_DIM, (g + 1) * HEAD_DIM)
        out = acc_ref[hs, :] / stats[g][1]
        o_ref[:, hs] = out.T.astype(o_ref.dtype)


def _attention_unshifted_kernel(qt_ref, k_ref, vt_ref, o_ref, acc_ref, s0_ref, *, tq, tk, n_kv_blocks):
    heads = [slice(g * HEAD_DIM, (g + 1) * HEAD_DIM) for g in range(GROUP)]
    n_trips = (qt_ref.shape[1] // tq) * n_kv_blocks
    acc_ref[...] = jnp.zeros_like(acc_ref)

    def offsets(t):
        return pl.multiple_of((t // n_kv_blocks) * tq, tq), pl.multiple_of((t % n_kv_blocks) * tk, tk)

    def scores(t, g):
        qoff, koff = offsets(t)
        return jnp.dot(k_ref[pl.ds(koff, tk), :], qt_ref[heads[g], pl.ds(qoff, tq)],
                       preferred_element_type=F32)

    s0_ref[...] = scores(0, 0)

    def body(t, carry):
        qoff, koff = offsets(t)
        vb = vt_ref[:, pl.ds(koff, tk)]
        new = []
        s = s0_ref[...]
        for g in range(GROUP):
            if g + 1 < GROUP:
                s_next = scores(t, g + 1)
            else:
                s_next = scores(jnp.minimum(t + 1, n_trips - 1), 0)
            p = jnp.exp2(s)
            new.append(carry[g] + jnp.sum(p, axis=0, keepdims=True))
            acc_ref[heads[g], :] += jnp.dot(vb, p.astype(BF16), preferred_element_type=F32)
            s = s_next
        s0_ref[...] = s
        tile_done = (t % n_kv_blocks) == n_kv_blocks - 1

        @pl.when(tile_done)
        def _():
            for g in range(GROUP):
                out = acc_ref[heads[g], :] / new[g]
                o_ref[pl.ds(qoff, tq), heads[g]] = out.T.astype(o_ref.dtype)
            acc_ref[...] = jnp.zeros_like(acc_ref)

        return tuple(jnp.where(tile_done, 0.0, l) for l in new)

    init = tuple(jnp.zeros((1, tq), F32) for _ in range(GROUP))
    lax.fori_loop(0, n_trips, body, init)


def _attention(qt, k, vt, n_seq, seq_len, body):
    t = k.shape[0]
    unshifted = body is _attention_unshifted_kernel
    tq = min(256, seq_len)
    gw = GROUP * HEAD_DIM
    scratch = [pltpu.VMEM((gw, tq), F32)]
    if unshifted:
        tk = min(8192, seq_len)
        bq = min(4096, seq_len)
        scratch.append(pltpu.VMEM((tk, tq), F32))
        kern = functools.partial(body, tq=tq, tk=tk, n_kv_blocks=seq_len // tk)
    else:
        tk = min(512, seq_len)
        bq = tq
        kern = functools.partial(body, tk=tk, n_kv_blocks=seq_len // tk)
    q_blocks = seq_len // bq
    return pl.pallas_call(
        kern,
        grid=(n_seq, N_KV, q_blocks),
        in_specs=[
            pl.BlockSpec((gw, bq), lambda b, h, i: (h, b * q_blocks + i)),
            pl.BlockSpec((seq_len, HEAD_DIM), lambda b, h, i: (b, h)),
            pl.BlockSpec((HEAD_DIM, seq_len), lambda b, h, i: (h, b)),
        ],
        out_specs=pl.BlockSpec((bq, gw), lambda b, h, i: (b * q_blocks + i, h)),
        out_shape=jax.ShapeDtypeStruct((t, D_ATTN), BF16),
        scratch_shapes=scratch,
        compiler_params=pltpu.CompilerParams(
            dimension_semantics=("parallel", "parallel", "arbitrary"), vmem_limit_bytes=VMEM_LIMIT),
        name="attention_unshifted" if unshifted else "attention_online",
    )(qt, k, vt)


def _conv_kernel(a_ref, b_ref, z_ref, ap_ref, bp_ref, an_ref, bn_ref,
                 w_ref, bias_ref, lng_ref, lnb_ref, o_ref, ext_ref, sh_ref, cv_ref, *, tiles_per_seq):
    tm = a_ref.shape[0]
    pos = pl.program_id(0) % tiles_per_seq

    def glu(a, b):
        return a.astype(F32) * _sigmoid(b.astype(F32))

    ext_ref[0:HALO, :] = jnp.where(pos != 0, glu(ap_ref[...], bp_ref[...]), 0.0)
    ext_ref[HALO:HALO + tm, :] = glu(a_ref[...], b_ref[...])
    ext_ref[HALO + tm:2 * HALO + tm, :] = jnp.where(
        pos != tiles_per_seq - 1, glu(an_ref[...], bn_ref[...]), 0.0)

    span = tm + SHIFT_SPAN
    sh_ref[0] = ext_ref[0:span, :]
    for s in range(1, SUBLANES):
        sh_ref[s] = ext_ref[s:s + span, :]

    base = HALO - CONV_PAD
    rows = min(tm, CONV_ACC_ROWS)
    for r0 in range(0, tm, rows):
        for c in range(D_CONV // LANES):
            cs = slice(c * LANES, (c + 1) * LANES)
            acc = jnp.zeros((rows, LANES), F32)
            for k in range(CONV_K):
                s, a = (base + k) % SUBLANES, (base + k) // SUBLANES * SUBLANES
                acc = acc + sh_ref[s, r0 + a:r0 + a + rows, cs] * w_ref[k:k + 1, cs]
            cv_ref[r0:r0 + rows, cs] = acc + bias_ref[:, cs]

    u = cv_ref[...]
    mu = jnp.mean(u, axis=-1, keepdims=True)
    uc = u - mu
    var = jnp.mean(uc * uc, axis=-1, keepdims=True)
    y = uc * lax.rsqrt(var + EPS) * lng_ref[...] + lnb_ref[...]
    o_ref[...] = (_silu(y) * _silu(z_ref[...].astype(F32))).astype(o_ref.dtype)


def _conv_branch(proj, dw, bias, lng, lnb, seq_len):
    t = proj.shape[0]
    tm = min(256, seq_len)
    tiles_per_seq = seq_len // tm
    hb = tm // HALO
    n_hb = t // HALO
    ca, cb, cz = OFF_GLU_A // D_CONV, OFF_GLU_B // D_CONV, OFF_ZCONV // D_CONV
    prev_map = lambda c: (lambda i: (jnp.maximum(i * hb - 1, 0), c))
    next_map = lambda c: (lambda i: (jnp.minimum((i + 1) * hb, n_hb - 1), c))
    row = lambda i: (0, 0)
    return pl.pallas_call(
        functools.partial(_conv_kernel, tiles_per_seq=tiles_per_seq),
        grid=(t // tm,),
        in_specs=[
            pl.BlockSpec((tm, D_CONV), lambda i: (i, ca)),
            pl.BlockSpec((tm, D_CONV), lambda i: (i, cb)),
            pl.BlockSpec((tm, D_CONV), lambda i: (i, cz)),
            pl.BlockSpec((HALO, D_CONV), prev_map(ca)),
            pl.BlockSpec((HALO, D_CONV), prev_map(cb)),
            pl.BlockSpec((HALO, D_CONV), next_map(ca)),
            pl.BlockSpec((HALO, D_CONV), next_map(cb)),
            pl.BlockSpec((CONV_K, D_CONV), row),
            pl.BlockSpec((1, D_CONV), row),
            pl.BlockSpec((1, D_CONV), row),
            pl.BlockSpec((1, D_CONV), row),
        ],
        out_specs=pl.BlockSpec((tm, D_CONV), lambda i: (i, 0)),
        out_shape=jax.ShapeDtypeStruct((t, D_CONV), BF16),
        scratch_shapes=[pltpu.VMEM((tm + 2 * HALO, D_CONV), F32),
                        pltpu.VMEM((SUBLANES, tm + SHIFT_SPAN, D_CONV), F32),
                        pltpu.VMEM((tm, D_CONV), F32)],
        compiler_params=pltpu.CompilerParams(
            dimension_semantics=("parallel",), vmem_limit_bytes=VMEM_LIMIT),
        name="conv_branch",
    )(proj, proj, proj, proj, proj, proj, proj, dw, bias, lng, lnb)


def _merge_kernel(u_ref, a_ref, z_ref, gc_ref, ga_ref, x_ref, wc_ref, wa_ref, wo_ref, pg_ref, o_ref):
    y_conv = jnp.dot(u_ref[...], wc_ref[...], preferred_element_type=F32)
    ag = (a_ref[...].astype(F32) * _silu(z_ref[...].astype(F32))).astype(BF16)
    y_attn = jnp.dot(ag, wa_ref[...], preferred_element_type=F32)
    m = _sigmoid(gc_ref[...].astype(F32)) * y_conv + _sigmoid(ga_ref[...].astype(F32)) * y_attn
    out = jnp.dot(m.astype(BF16), wo_ref[...], preferred_element_type=F32)
    ms = jnp.mean(out * out, axis=-1, keepdims=True)
    o_ref[...] = x_ref[...] + out * lax.rsqrt(ms + EPS) * pg_ref[...]


def _merge_out(u, a, proj, x, wc, wa, wo, pg):
    t = x.shape[0]
    tm = min(256, t)
    const = lambda i: (0, 0)
    resident = functools.partial(pl.BlockSpec, index_map=const, pipeline_mode=pl.Buffered(1))
    return pl.pallas_call(
        _merge_kernel,
        grid=(t // tm,),
        in_specs=[
            pl.BlockSpec((tm, D_CONV), lambda i: (i, 0)),
            pl.BlockSpec((tm, D_ATTN), lambda i: (i, 0)),
            pl.BlockSpec((tm, D_ATTN), lambda i: (i, OFF_ZATTN // D_ATTN)),
            pl.BlockSpec((tm, D_MODEL), lambda i: (i, OFF_GCONV // D_MODEL)),
            pl.BlockSpec((tm, D_MODEL), lambda i: (i, OFF_GATTN // D_MODEL)),
            pl.BlockSpec((tm, D_MODEL), lambda i: (i, 0)),
            resident((D_CONV, D_MODEL)),
            resident((D_ATTN, D_MODEL)),
            resident((D_MODEL, D_MODEL)),
            pl.BlockSpec((1, D_MODEL), const),
        ],
        out_specs=pl.BlockSpec((tm, D_MODEL), lambda i: (i, 0)),
        out_shape=jax.ShapeDtypeStruct((t, D_MODEL), F32),
        compiler_params=pltpu.CompilerParams(
            dimension_semantics=("parallel",), vmem_limit_bytes=VMEM_LIMIT),
        name="merge_out",
    )(u, a, proj, proj, proj, x, wc, wa, wo, pg)


def _pair_major(w):
    lead = w.shape[:-1]
    n = w.shape[-1] // HEAD_DIM
    w = w.reshape(lead + (n, 2, 2, ROPE_FREQS))
    w = jnp.swapaxes(w, -2, -3)
    return w.reshape(lead + (n * HEAD_DIM,))


def _pack_w_in(w_in):
    o = [0, D_CONV, 2 * D_CONV, 3 * D_CONV]
    o.append(o[-1] + D_ATTN)
    o.append(o[-1] + D_KV)
    o.append(o[-1] + D_KV)
    o.append(o[-1] + D_ATTN)
    o.append(o[-1] + D_MODEL)
    o.append(o[-1] + D_MODEL)
    glu_a, glu_b, z_conv, q, k, v, z_attn, g_conv, g_attn = (
        w_in[..., o[n]:o[n + 1]] for n in range(9))
    packed = jnp.concatenate(
        [_pair_major(q), z_attn, g_conv, g_attn, glu_a, glu_b, z_conv, _pair_major(k), v], axis=-1)
    return packed.astype(BF16)


def _rope_tables(seq_len):
    rows = seq_len // GRID_W
    row_idx = jnp.broadcast_to(jnp.arange(rows, dtype=F32)[:, None], (rows, GRID_W)).reshape(-1)
    col_idx = jnp.broadcast_to(jnp.arange(GRID_W, dtype=F32)[None, :], (rows, GRID_W)).reshape(-1)
    inv_freq = ROPE_THETA ** (-(jnp.arange(ROPE_FREQS, dtype=F32) * 2.0) / ROPE_AXIS_DIM)
    ar = row_idx[:, None] * inv_freq
    ac = col_idx[:, None] * inv_freq
    cos = jnp.concatenate([jnp.cos(ar), jnp.cos(ac), jnp.cos(ar), jnp.cos(ac)], axis=-1)
    sin = jnp.concatenate([-jnp.sin(ar), -jnp.sin(ac), jnp.sin(ar), jnp.sin(ac)], axis=-1)
    return cos, sin


def _layer(x, n_seq, seq_len, tables, p):
    cos, sin = tables
    proj = _inproj(x, p["pre_g"], p["w_in"])
    qt, k, vt = _qkv_prep(proj, cos, sin, p["gq"], p["gk"], seq_len)
    score_bound = HEAD_DIM * Q_SCALE * jnp.max(jnp.abs(p["gq"])) * jnp.max(jnp.abs(p["gk"]))
    a = lax.cond(
        score_bound <= MAX_UNSHIFTED,
        functools.partial(_attention, n_seq=n_seq, seq_len=seq_len, body=_attention_unshifted_kernel),
        functools.partial(_attention, n_seq=n_seq, seq_len=seq_len, body=_attention_kernel),
        qt, k, vt)
    u = _conv_branch(proj, p["dw"], p["dw_b"], p["ln_g"], p["ln_b"], seq_len)
    return _merge_out(u, a, proj, x, p["w_conv_out"], p["w_attn_o"], p["w_out"], p["post_g"])


def kernel(x_prompt, x_sample, pre_norm_g, w_in, dw_kernel, dw_bias, conv_ln_g, conv_ln_b,
           w_conv_out, q_norm_g, k_norm_g, w_attn_o, w_out, post_norm_g):
    depth = w_in.shape[0]
    w_in_p = _pack_w_in(w_in)
    wc = w_conv_out.astype(BF16)
    wa = w_attn_o.astype(BF16)
    wo = w_out.astype(BF16)
    gq = _pair_major(q_norm_g)
    gk = _pair_major(k_norm_g)

    streams = []
    for x in (x_prompt, x_sample):
        b, s, d = x.shape
        streams.append((x.reshape(b * s, d), b, s, _rope_tables(s)))

    outs = []
    for x2, b, s, tables in streams:
        y = x2
        for l in range(depth):
            p = dict(
                pre_g=pre_norm_g[l][None, :], w_in=w_in_p[l], dw=dw_kernel[l, :, 0, :],
                dw_b=dw_bias[l][None, :], ln_g=conv_ln_g[l][None, :], ln_b=conv_ln_b[l][None, :],
                w_conv_out=wc[l], gq=gq[l][None, :], gk=gk[l][None, :], w_attn_o=wa[l],
                w_out=wo[l], post_g=post_norm_g[l][None, :])
            y = _layer(y, b, s, tables, p)
        outs.append(y.reshape(b, s, d))
    return tuple(outs)
```

```python
import functools
import math

import jax
import jax.numpy as jnp
from jax import lax
from jax.experimental import pallas as pl
from jax.experimental.pallas import tpu as pltpu

F32 = jnp.float32
BF16 = jnp.bfloat16

D_MODEL = 2048
DEPTH = 4
GRID_W = 64
D_CONV = D_MODEL // 2
CONV_K = 31
CONV_PAD = CONV_K // 2
HEAD_DIM = 128
N_HEADS = D_MODEL // HEAD_DIM
N_KV = 4
GROUP = N_HEADS // N_KV
D_ATTN = N_HEADS * HEAD_DIM
D_KV = N_KV * HEAD_DIM
ROPE_AXIS_DIM = HEAD_DIM // 2
ROPE_FREQS = ROPE_AXIS_DIM // 2
ROPE_THETA = 10000.0
EPS = 1e-6
N_IN = 3 * D_CONV + 2 * D_ATTN + 2 * D_KV + 2 * D_MODEL

OFF_Q = 0
OFF_ZATTN = OFF_Q + D_ATTN
OFF_GCONV = OFF_ZATTN + D_ATTN
OFF_GATTN = OFF_GCONV + D_MODEL
OFF_GLU_A = OFF_GATTN + D_MODEL
OFF_GLU_B = OFF_GLU_A + D_CONV
OFF_ZCONV = OFF_GLU_B + D_CONV
OFF_K = OFF_ZCONV + D_CONV
OFF_V = OFF_K + D_KV

HALO = 16
LANES = 128
SUBLANES = 8
CONV_ACC_ROWS = 128
SHIFT_SPAN = (HALO - CONV_PAD + CONV_K - 1) // SUBLANES * SUBLANES
INPROJ_DOT_COLS = 1024
VMEM_LIMIT = 56 * 1024 * 1024
NEG_BIG = -1e30
MAX_UNSHIFTED = 60.0
Q_SCALE = (1.0 / math.sqrt(HEAD_DIM)) * math.log2(math.e)


def _sigmoid(x):
    return 1.0 / (1.0 + jnp.exp(-x))


def _silu(x):
    return x * _sigmoid(x)


def _inproj_kernel(x_ref, g_ref, w_ref, o_ref, h_ref):
    @pl.when(pl.program_id(1) == 0)
    def _():
        x = x_ref[...]
        ms = jnp.mean(x * x, axis=-1, keepdims=True)
        h_ref[...] = (x * lax.rsqrt(ms + EPS) * g_ref[...]).astype(BF16)

    for c in range(0, o_ref.shape[1], INPROJ_DOT_COLS):
        cs = slice(c, c + INPROJ_DOT_COLS)
        o_ref[:, cs] = jnp.dot(h_ref[...], w_ref[:, cs], preferred_element_type=F32).astype(o_ref.dtype)


def _inproj(x, g, w):
    t = x.shape[0]
    tm = min(1024, t)
    tn = 2 * INPROJ_DOT_COLS
    return pl.pallas_call(
        _inproj_kernel,
        grid=(t // tm, N_IN // tn),
        in_specs=[
            pl.BlockSpec((tm, D_MODEL), lambda i, j: (i, 0)),
            pl.BlockSpec((1, D_MODEL), lambda i, j: (0, 0)),
            pl.BlockSpec((D_MODEL, tn), lambda i, j: (0, j)),
        ],
        out_specs=pl.BlockSpec((tm, tn), lambda i, j: (i, j)),
        out_shape=jax.ShapeDtypeStruct((t, N_IN), BF16),
        scratch_shapes=[pltpu.VMEM((tm, D_MODEL), BF16)],
        compiler_params=pltpu.CompilerParams(
            dimension_semantics=("parallel", "arbitrary"), vmem_limit_bytes=VMEM_LIMIT),
        name="inproj",
    )(x, g, w)


def _norm_rope(x, g, cos, sin):
    ms = jnp.mean(x * x, axis=-1, keepdims=True)
    y = x * lax.rsqrt(ms + EPS) * g
    return y * cos + pltpu.roll(y, HEAD_DIM // 2, axis=1) * sin


def _qkv_prep_kernel(q_ref, k_ref, v_ref, cos_ref, sin_ref, gq_ref, gk_ref,
                     qt_ref, ko_ref, vt_ref):
    cos = cos_ref[...]
    sin = sin_ref[...]
    gq = gq_ref[...] * Q_SCALE
    gk = gk_ref[...]
    for h in range(N_HEADS):
        hs = slice(h * HEAD_DIM, (h + 1) * HEAD_DIM)
        y = _norm_rope(q_ref[:, hs].astype(F32), gq, cos, sin)
        qt_ref[hs, :] = y.T.astype(BF16)
    for h in range(N_KV):
        hs = slice(h * HEAD_DIM, (h + 1) * HEAD_DIM)
        y = _norm_rope(k_ref[:, hs].astype(F32), gk, cos, sin)
        ko_ref[:, hs] = y.astype(BF16)
        vt_ref[hs, :] = v_ref[:, hs].astype(F32).T.astype(BF16)


def _qkv_prep(proj, cos, sin, gq, gk, seq_len):
    t = proj.shape[0]
    tm = min(512, seq_len)
    tiles_per_seq = seq_len // tm
    return pl.pallas_call(
        _qkv_prep_kernel,
        grid=(t // tm,),
        in_specs=[
            pl.BlockSpec((tm, D_ATTN), lambda i: (i, OFF_Q // D_ATTN)),
            pl.BlockSpec((tm, D_KV), lambda i: (i, OFF_K // D_KV)),
            pl.BlockSpec((tm, D_KV), lambda i: (i, OFF_V // D_KV)),
            pl.BlockSpec((tm, HEAD_DIM), lambda i: (i % tiles_per_seq, 0)),
            pl.BlockSpec((tm, HEAD_DIM), lambda i: (i % tiles_per_seq, 0)),
            pl.BlockSpec((1, HEAD_DIM), lambda i: (0, 0)),
            pl.BlockSpec((1, HEAD_DIM), lambda i: (0, 0)),
        ],
        out_specs=[
            pl.BlockSpec((D_ATTN, tm), lambda i: (0, i)),
            pl.BlockSpec((tm, D_KV), lambda i: (i, 0)),
            pl.BlockSpec((D_KV, tm), lambda i: (0, i)),
        ],
        out_shape=[
            jax.ShapeDtypeStruct((D_ATTN, t), BF16),
            jax.ShapeDtypeStruct((t, D_KV), BF16),
            jax.ShapeDtypeStruct((D_KV, t), BF16),
        ],
        compiler_params=pltpu.CompilerParams(
            dimension_semantics=("parallel",), vmem_limit_bytes=VMEM_LIMIT),
        name="qkv_prep",
    )(proj, proj, proj, cos, sin, gq, gk)


def _attention_kernel(qt_ref, k_ref, vt_ref, o_ref, acc_ref, *, tk, n_kv_blocks):
    tq = qt_ref.shape[1]
    acc_ref[...] = jnp.zeros_like(acc_ref)

    def body(j, carry):
        off = pl.multiple_of(j * tk, tk)
        kb = k_ref[pl.ds(off, tk), :]
        vb = vt_ref[:, pl.ds(off, tk)]
        new = []
        for g in range(GROUP):
            m, l = carry[g]
            hs = slice(g * HEAD_DIM, (g + 1) * HEAD_DIM)
            s = jnp.dot(kb, qt_ref[hs, :], preferred_element_type=F32)
            m_new = jnp.maximum(m, jnp.max(s, axis=0, keepdims=True))
            alpha = jnp.exp2(m - m_new)
            p = jnp.exp2(s - m_new)
            l_new = alpha * l + jnp.sum(p, axis=0, keepdims=True)
            pv = jnp.dot(vb, p.astype(BF16), preferred_element_type=F32)
            acc_ref[hs, :] = alpha * acc_ref[hs, :] + pv
            new.append((m_new, l_new))
        return tuple(new)

    init = tuple((jnp.full((1, tq), NEG_BIG, F32), jnp.zeros((1, tq), F32)) for _ in range(GROUP))
    stats = lax.fori_loop(0, n_kv_blocks, body, init)
    for g in range(GROUP):
        hs = slice(g * HEAD_DIM, (g + 1) * HEAD_DIM)
        out = acc_ref[hs, :] / stats[g][1]
        o_ref[:, hs] = out.T.astype(o_ref.dtype)


def _attention_unshifted_kernel(qt_ref, k_ref, vt_ref, o_ref, acc_ref, s0_ref, *, tq, tk, n_kv_blocks):
    heads = [slice(g * HEAD_DIM, (g + 1) * HEAD_DIM) for g in range(GROUP)]
    n_trips = (qt_ref.shape[1] // tq) * n_kv_blocks
    acc_ref[...] = jnp.zeros_like(acc_ref)

    def offsets(t):
        return pl.multiple_of((t // n_kv_blocks) * tq, tq), pl.multiple_of((t % n_kv_blocks) * tk, tk)

    def scores(t, g):
        qoff, koff = offsets(t)
        return jnp.dot(k_ref[pl.ds(koff, tk), :], qt_ref[heads[g], pl.ds(qoff, tq)],
                       preferred_element_type=F32)

    s0_ref[...] = scores(0, 0)

    def body(t, carry):
        qoff, koff = offsets(t)
        vb = vt_ref[:, pl.ds(koff, tk)]
        new = []
        s = s0_ref[...]
        for g in range(GROUP):
            if g + 1 < GROUP:
                s_next = scores(t, g + 1)
            else:
                s_next = scores(jnp.minimum(t + 1, n_trips - 1), 0)
            p = jnp.exp2(s)
            new.append(carry[g] + jnp.sum(p, axis=0, keepdims=True))
            acc_ref[heads[g], :] += jnp.dot(vb, p.astype(BF16), preferred_element_type=F32)
            s = s_next
        s0_ref[...] = s
        tile_done = (t % n_kv_blocks) == n_kv_blocks - 1

        @pl.when(tile_done)
        def _():
            for g in range(GROUP):
                out = acc_ref[heads[g], :] / new[g]
                o_ref[pl.ds(qoff, tq), heads[g]] = out.T.astype(o_ref.dtype)
            acc_ref[...] = jnp.zeros_like(acc_ref)

        return tuple(jnp.where(tile_done, 0.0, l) for l in new)

    init = tuple(jnp.zeros((1, tq), F32) for _ in range(GROUP))
    lax.fori_loop(0, n_trips, body, init)


def _attention(qt, k, vt, n_seq, seq_len, body):
    t = k.shape[0]
    unshifted = body is _attention_unshifted_kernel
    tq = min(256, seq_len)
    gw = GROUP * HEAD_DIM
    scratch = [pltpu.VMEM((gw, tq), F32)]
    if unshifted:
        tk = min(8192, seq_len)
        bq = min(4096, seq_len)
        scratch.append(pltpu.VMEM((tk, tq), F32))
        kern = functools.partial(body, tq=tq, tk=tk, n_kv_blocks=seq_len // tk)
    else:
        tk = min(512, seq_len)
        bq = tq
        kern = functools.partial(body, tk=tk, n_kv_blocks=seq_len // tk)
    q_blocks = seq_len // bq
    return pl.pallas_call(
        kern,
        grid=(n_seq, N_KV, q_blocks),
        in_specs=[
            pl.BlockSpec((gw, bq), lambda b, h, i: (h, b * q_blocks + i)),
            pl.BlockSpec((seq_len, HEAD_DIM), lambda b, h, i: (b, h)),
            pl.BlockSpec((HEAD_DIM, seq_len), lambda b, h, i: (h, b)),
        ],
        out_specs=pl.BlockSpec((bq, gw), lambda b, h, i: (b * q_blocks + i, h)),
        out_shape=jax.ShapeDtypeStruct((t, D_ATTN), BF16),
        scratch_shapes=scratch,
        compiler_params=pltpu.CompilerParams(
            dimension_semantics=("parallel", "parallel", "arbitrary"), vmem_limit_bytes=VMEM_LIMIT),
        name="attention_unshifted" if unshifted else "attention_online",
    )(qt, k, vt)


def _conv_kernel(a_ref, b_ref, z_ref, ap_ref, bp_ref, an_ref, bn_ref,
                 w_ref, bias_ref, lng_ref, lnb_ref, o_ref, ext_ref, sh_ref, cv_ref, *, tiles_per_seq):
    tm = a_ref.shape[0]
    pos = pl.program_id(0) % tiles_per_seq

    def glu(a, b):
        return a.astype(F32) * _sigmoid(b.astype(F32))

    ext_ref[0:HALO, :] = jnp.where(pos != 0, glu(ap_ref[...], bp_ref[...]), 0.0)
    ext_ref[HALO:HALO + tm, :] = glu(a_ref[...], b_ref[...])
    ext_ref[HALO + tm:2 * HALO + tm, :] = jnp.where(
        pos != tiles_per_seq - 1, glu(an_ref[...], bn_ref[...]), 0.0)

    span = tm + SHIFT_SPAN
    sh_ref[0] = ext_ref[0:span, :]
    for s in range(1, SUBLANES):
        sh_ref[s] = ext_ref[s:s + span, :]

    base = HALO - CONV_PAD
    rows = min(tm, CONV_ACC_ROWS)
    for r0 in range(0, tm, rows):
        for c in range(D_CONV // LANES):
            cs = slice(c * LANES, (c + 1) * LANES)
            acc = jnp.zeros((rows, LANES), F32)
            for k in range(CONV_K):
                s, a = (base + k) % SUBLANES, (base + k) // SUBLANES * SUBLANES
                acc = acc + sh_ref[s, r0 + a:r0 + a + rows, cs] * w_ref[k:k + 1, cs]
            cv_ref[r0:r0 + rows, cs] = acc + bias_ref[:, cs]

    u = cv_ref[...]
    mu = jnp.mean(u, axis=-1, keepdims=True)
    uc = u - mu
    var = jnp.mean(uc * uc, axis=-1, keepdims=True)
    y = uc * lax.rsqrt(var + EPS) * lng_ref[...] + lnb_ref[...]
    o_ref[...] = (_silu(y) * _silu(z_ref[...].astype(F32))).astype(o_ref.dtype)


def _conv_branch(proj, dw, bias, lng, lnb, seq_len):
    t = proj.shape[0]
    tm = min(256, seq_len)
    tiles_per_seq = seq_len // tm
    hb = tm // HALO
    n_hb = t // HALO
    ca, cb, cz = OFF_GLU_A // D_CONV, OFF_GLU_B // D_CONV, OFF_ZCONV // D_CONV
    prev_map = lambda c: (lambda i: (jnp.maximum(i * hb - 1, 0), c))
    next_map = lambda c: (lambda i: (jnp.minimum((i + 1) * hb, n_hb - 1), c))
    row = lambda i: (0, 0)
    return pl.pallas_call(
        functools.partial(_conv_kernel, tiles_per_seq=tiles_per_seq),
        grid=(t // tm,),
        in_specs=[
            pl.BlockSpec((tm, D_CONV), lambda i: (i, ca)),
            pl.BlockSpec((tm, D_CONV), lambda i: (i, cb)),
            pl.BlockSpec((tm, D_CONV), lambda i: (i, cz)),
            pl.BlockSpec((HALO, D_CONV), prev_map(ca)),
            pl.BlockSpec((HALO, D_CONV), prev_map(cb)),
            pl.BlockSpec((HALO, D_CONV), next_map(ca)),
            pl.BlockSpec((HALO, D_CONV), next_map(cb)),
            pl.BlockSpec((CONV_K, D_CONV), row),
            pl.BlockSpec((1, D_CONV), row),
            pl.BlockSpec((1, D_CONV), row),
            pl.BlockSpec((1, D_CONV), row),
        ],
        out_specs=pl.BlockSpec((tm, D_CONV), lambda i: (i, 0)),
        out_shape=jax.ShapeDtypeStruct((t, D_CONV), BF16),
        scratch_shapes=[pltpu.VMEM((tm + 2 * HALO, D_CONV), F32),
                        pltpu.VMEM((SUBLANES, tm + SHIFT_SPAN, D_CONV), F32),
                        pltpu.VMEM((tm, D_CONV), F32)],
        compiler_params=pltpu.CompilerParams(
            dimension_semantics=("parallel",), vmem_limit_bytes=VMEM_LIMIT),
        name="conv_branch",
    )(proj, proj, proj, proj, proj, proj, proj, dw, bias, lng, lnb)


def _merge_kernel(u_ref, a_ref, z_ref, gc_ref, ga_ref, x_ref, wc_ref, wa_ref, wo_ref, pg_ref, o_ref):
    y_conv = jnp.dot(u_ref[...], wc_ref[...], preferred_element_type=F32)
    ag = (a_ref[...].astype(F32) * _silu(z_ref[...].astype(F32))).astype(BF16)
    y_attn = jnp.dot(ag, wa_ref[...], preferred_element_type=F32)
    m = _sigmoid(gc_ref[...].astype(F32)) * y_conv + _sigmoid(ga_ref[...].astype(F32)) * y_attn
    out = jnp.dot(m.astype(BF16), wo_ref[...], preferred_element_type=F32)
    ms = jnp.mean(out * out, axis=-1, keepdims=True)
    o_ref[...] = x_ref[...] + out * lax.rsqrt(ms + EPS) * pg_ref[...]


def _merge_out(u, a, proj, x, wc, wa, wo, pg):
    t = x.shape[0]
    tm = min(256, t)
    const = lambda i: (0, 0)
    resident = functools.partial(pl.BlockSpec, index_map=const, pipeline_mode=pl.Buffered(1))
    return pl.pallas_call(
        _merge_kernel,
        grid=(t // tm,),
        in_specs=[
            pl.BlockSpec((tm, D_CONV), lambda i: (i, 0)),
            pl.BlockSpec((tm, D_ATTN), lambda i: (i, 0)),
            pl.BlockSpec((tm, D_ATTN), lambda i: (i, OFF_ZATTN // D_ATTN)),
            pl.BlockSpec((tm, D_MODEL), lambda i: (i, OFF_GCONV // D_MODEL)),
            pl.BlockSpec((tm, D_MODEL), lambda i: (i, OFF_GATTN // D_MODEL)),
            pl.BlockSpec((tm, D_MODEL), lambda i: (i, 0)),
            resident((D_CONV, D_MODEL)),
            resident((D_ATTN, D_MODEL)),
            resident((D_MODEL, D_MODEL)),
            pl.BlockSpec((1, D_MODEL), const),
        ],
        out_specs=pl.BlockSpec((tm, D_MODEL), lambda i: (i, 0)),
        out_shape=jax.ShapeDtypeStruct((t, D_MODEL), F32),
        compiler_params=pltpu.CompilerParams(
            dimension_semantics=("parallel",), vmem_limit_bytes=VMEM_LIMIT),
        name="merge_out",
    )(u, a, proj, proj, proj, x, wc, wa, wo, pg)


def _pair_major(w):
    lead = w.shape[:-1]
    n = w.shape[-1] // HEAD_DIM
    w = w.reshape(lead + (n, 2, 2, ROPE_FREQS))
    w = jnp.swapaxes(w, -2, -3)
    return w.reshape(lead + (n * HEAD_DIM,))


def _pack_w_in(w_in):
    o = [0, D_CONV, 2 * D_CONV, 3 * D_CONV]
    o.append(o[-1] + D_ATTN)
    o.append(o[-1] + D_KV)
    o.append(o[-1] + D_KV)
    o.append(o[-1] + D_ATTN)
    o.append(o[-1] + D_MODEL)
    o.append(o[-1] + D_MODEL)
    w_in = w_in.astype(BF16)
    glu_a, glu_b, z_conv, q, k, v, z_attn, g_conv, g_attn = (
        w_in[..., o[n]:o[n + 1]] for n in range(9))
    return jnp.concatenate(
        [_pair_major(q), z_attn, g_conv, g_attn, glu_a, glu_b, z_conv, _pair_major(k), v], axis=-1)


def _rope_tables(seq_len):
    rows = seq_len // GRID_W
    row_idx = jnp.broadcast_to(jnp.arange(rows, dtype=F32)[:, None], (rows, GRID_W)).reshape(-1)
    col_idx = jnp.broadcast_to(jnp.arange(GRID_W, dtype=F32)[None, :], (rows, GRID_W)).reshape(-1)
    inv_freq = ROPE_THETA ** (-(jnp.arange(ROPE_FREQS, dtype=F32) * 2.0) / ROPE_AXIS_DIM)
    ar = row_idx[:, None] * inv_freq
    ac = col_idx[:, None] * inv_freq
    cos = jnp.concatenate([jnp.cos(ar), jnp.cos(ac), jnp.cos(ar), jnp.cos(ac)], axis=-1)
    sin = jnp.concatenate([-jnp.sin(ar), -jnp.sin(ac), jnp.sin(ar), jnp.sin(ac)], axis=-1)
    return cos, sin


def _layer(x, n_seq, seq_len, tables, p):
    cos, sin = tables
    proj = _inproj(x, p["pre_g"], p["w_in"])
    qt, k, vt = _qkv_prep(proj, cos, sin, p["gq"], p["gk"], seq_len)
    score_bound = HEAD_DIM * Q_SCALE * jnp.max(jnp.abs(p["gq"])) * jnp.max(jnp.abs(p["gk"]))
    a = lax.cond(
        score_bound <= MAX_UNSHIFTED,
        functools.partial(_attention, n_seq=n_seq, seq_len=seq_len, body=_attention_unshifted_kernel),
        functools.partial(_attention, n_seq=n_seq, seq_len=seq_len, body=_attention_kernel),
        qt, k, vt)
    u = _conv_branch(proj, p["dw"], p["dw_b"], p["ln_g"], p["ln_b"], seq_len)
    return _merge_out(u, a, proj, x, p["w_conv_out"], p["w_attn_o"], p["w_out"], p["post_g"])


def kernel(x_prompt, x_sample, pre_norm_g, w_in, dw_kernel, dw_bias, conv_ln_g, conv_ln_b,
           w_conv_out, q_norm_g, k_norm_g, w_attn_o, w_out, post_norm_g):
    depth = w_in.shape[0]
    w_in_p = _pack_w_in(w_in)
    wc = w_conv_out.astype(BF16)
    wa = w_attn_o.astype(BF16)
    wo = w_out.astype(BF16)
    gq = _pair_major(q_norm_g)
    gk = _pair_major(k_norm_g)

    streams = []
    for x in (x_prompt, x_sample):
        b, s, d = x.shape
        streams.append((x.reshape(b * s, d), b, s, _rope_tables(s)))

    outs = []
    for x2, b, s, tables in streams:
        y = x2
        for l in range(depth):
            p = dict(
                pre_g=pre_norm_g[l][None, :], w_in=w_in_p[l], dw=dw_kernel[l, :, 0, :],
                dw_b=dw_bias[l][None, :], ln_g=conv_ln_g[l][None, :], ln_b=conv_ln_b[l][None, :],
                w_conv_out=wc[l], gq=gq[l][None, :], gk=gk[l][None, :], w_attn_o=wa[l],
                w_out=wo[l], post_g=post_norm_g[l][None, :])
            y = _layer(y, b, s, tables, p)
        outs.append(y.reshape(b, s, d))
    return tuple(outs)
```

```python
import functools
import math

import jax
import jax.numpy as jnp
from jax import lax
from jax.experimental import pallas as pl
from jax.experimental.pallas import tpu as pltpu

F32 = jnp.float32
BF16 = jnp.bfloat16

D_MODEL = 2048
DEPTH = 4
GRID_W = 64
D_CONV = D_MODEL // 2
CONV_K = 31
CONV_PAD = CONV_K // 2
HEAD_DIM = 128
N_HEADS = D_MODEL // HEAD_DIM
N_KV = 4
GROUP = N_HEADS // N_KV
D_ATTN = N_HEADS * HEAD_DIM
D_KV = N_KV * HEAD_DIM
ROPE_AXIS_DIM = HEAD_DIM // 2
ROPE_FREQS = ROPE_AXIS_DIM // 2
ROPE_THETA = 10000.0
EPS = 1e-6
N_IN = 3 * D_CONV + 2 * D_ATTN + 2 * D_KV + 2 * D_MODEL

OFF_Q = 0
OFF_ZATTN = OFF_Q + D_ATTN
OFF_GCONV = OFF_ZATTN + D_ATTN
OFF_GATTN = OFF_GCONV + D_MODEL
OFF_GLU_A = OFF_GATTN + D_MODEL
OFF_GLU_B = OFF_GLU_A + D_CONV
OFF_ZCONV = OFF_GLU_B + D_CONV
OFF_K = OFF_ZCONV + D_CONV
OFF_V = OFF_K + D_KV

HALO = 16
LANES = 128
SUBLANES = 8
CONV_ACC_ROWS = 128
SHIFT_SPAN = (HALO - CONV_PAD + CONV_K - 1) // SUBLANES * SUBLANES
INPROJ_DOT_COLS = 1024
VMEM_LIMIT = 56 * 1024 * 1024
NEG_BIG = -1e30
MAX_UNSHIFTED = 60.0
NT_DIMS = (((1,), (1,)), ((), ()))
Q_SCALE = (1.0 / math.sqrt(HEAD_DIM)) * math.log2(math.e)


def _sigmoid(x):
    return 1.0 / (1.0 + jnp.exp(-x))


def _silu(x):
    return x * _sigmoid(x)


def _inproj_kernel(x_ref, g_ref, w_ref, o_ref, h_ref):
    @pl.when(pl.program_id(1) == 0)
    def _():
        x = x_ref[...]
        ms = jnp.mean(x * x, axis=-1, keepdims=True)
        h_ref[...] = (x * lax.rsqrt(ms + EPS) * g_ref[...]).astype(BF16)

    for c in range(0, o_ref.shape[1], INPROJ_DOT_COLS):
        cs = slice(c, c + INPROJ_DOT_COLS)
        o_ref[:, cs] = jnp.dot(h_ref[...], w_ref[:, cs], preferred_element_type=F32).astype(o_ref.dtype)


def _inproj(x, g, w):
    t = x.shape[0]
    tm = min(1024, t)
    tn = 2 * INPROJ_DOT_COLS
    return pl.pallas_call(
        _inproj_kernel,
        grid=(t // tm, N_IN // tn),
        in_specs=[
            pl.BlockSpec((tm, D_MODEL), lambda i, j: (i, 0)),
            pl.BlockSpec((1, D_MODEL), lambda i, j: (0, 0)),
            pl.BlockSpec((D_MODEL, tn), lambda i, j: (0, j)),
        ],
        out_specs=pl.BlockSpec((tm, tn), lambda i, j: (i, j)),
        out_shape=jax.ShapeDtypeStruct((t, N_IN), BF16),
        scratch_shapes=[pltpu.VMEM((tm, D_MODEL), BF16)],
        compiler_params=pltpu.CompilerParams(
            dimension_semantics=("parallel", "arbitrary"), vmem_limit_bytes=VMEM_LIMIT),
        name="inproj",
    )(x, g, w)


def _norm_rope(x, g, cos, sin):
    ones = jnp.ones((HEAD_DIM, HEAD_DIM), F32)
    ms = jnp.dot(x * x, ones, preferred_element_type=F32) * (1.0 / HEAD_DIM)
    y = x * lax.rsqrt(ms + EPS) * g
    return y * cos + pltpu.roll(y, HEAD_DIM // 2, axis=1) * sin


def _qkv_prep_kernel(q_ref, k_ref, v_ref, cos_ref, sin_ref, gq_ref, gk_ref,
                     qo_ref, ko_ref, vt_ref):
    cos = cos_ref[...]
    sin = sin_ref[...]
    gq = gq_ref[...] * Q_SCALE
    gk = gk_ref[...]
    for h in range(N_HEADS):
        hs = slice(h * HEAD_DIM, (h + 1) * HEAD_DIM)
        y = _norm_rope(q_ref[:, hs].astype(F32), gq, cos, sin)
        qo_ref[:, hs] = y.astype(BF16)
    for h in range(N_KV):
        hs = slice(h * HEAD_DIM, (h + 1) * HEAD_DIM)
        y = _norm_rope(k_ref[:, hs].astype(F32), gk, cos, sin)
        ko_ref[:, hs] = y.astype(BF16)
        vt_ref[hs, :] = v_ref[:, hs].astype(F32).T.astype(BF16)


def _qkv_prep(proj, cos, sin, gq, gk, seq_len):
    t = proj.shape[0]
    tm = min(512, seq_len)
    tiles_per_seq = seq_len // tm
    return pl.pallas_call(
        _qkv_prep_kernel,
        grid=(t // tm,),
        in_specs=[
            pl.BlockSpec((tm, D_ATTN), lambda i: (i, OFF_Q // D_ATTN)),
            pl.BlockSpec((tm, D_KV), lambda i: (i, OFF_K // D_KV)),
            pl.BlockSpec((tm, D_KV), lambda i: (i, OFF_V // D_KV)),
            pl.BlockSpec((tm, HEAD_DIM), lambda i: (i % tiles_per_seq, 0)),
            pl.BlockSpec((tm, HEAD_DIM), lambda i: (i % tiles_per_seq, 0)),
            pl.BlockSpec((1, HEAD_DIM), lambda i: (0, 0)),
            pl.BlockSpec((1, HEAD_DIM), lambda i: (0, 0)),
        ],
        out_specs=[
            pl.BlockSpec((tm, D_ATTN), lambda i: (i, 0)),
            pl.BlockSpec((tm, D_KV), lambda i: (i, 0)),
            pl.BlockSpec((D_KV, tm), lambda i: (0, i)),
        ],
        out_shape=[
            jax.ShapeDtypeStruct((t, D_ATTN), BF16),
            jax.ShapeDtypeStruct((t, D_KV), BF16),
            jax.ShapeDtypeStruct((D_KV, t), BF16),
        ],
        compiler_params=pltpu.CompilerParams(
            dimension_semantics=("parallel",), vmem_limit_bytes=VMEM_LIMIT),
        name="qkv_prep",
    )(proj, proj, proj, cos, sin, gq, gk)


def _attention_kernel(q_ref, k_ref, vt_ref, o_ref, acc_ref, *, tk, n_kv_blocks):
    tq = q_ref.shape[0]
    acc_ref[...] = jnp.zeros_like(acc_ref)

    def body(j, carry):
        off = pl.multiple_of(j * tk, tk)
        kb = k_ref[pl.ds(off, tk), :]
        vb = vt_ref[:, pl.ds(off, tk)]
        new = []
        for g in range(GROUP):
            m, l = carry[g]
            hs = slice(g * HEAD_DIM, (g + 1) * HEAD_DIM)
            s = lax.dot_general(kb, q_ref[:, hs], NT_DIMS, preferred_element_type=F32)
            m_new = jnp.maximum(m, jnp.max(s, axis=0, keepdims=True))
            alpha = jnp.exp2(m - m_new)
            p = jnp.exp2(s - m_new)
            l_new = alpha * l + jnp.sum(p, axis=0, keepdims=True)
            pv = jnp.dot(vb, p.astype(BF16), preferred_element_type=F32)
            acc_ref[hs, :] = alpha * acc_ref[hs, :] + pv
            new.append((m_new, l_new))
        return tuple(new)

    init = tuple((jnp.full((1, tq), NEG_BIG, F32), jnp.zeros((1, tq), F32)) for _ in range(GROUP))
    stats = lax.fori_loop(0, n_kv_blocks, body, init)
    for g in range(GROUP):
        hs = slice(g * HEAD_DIM, (g + 1) * HEAD_DIM)
        out = acc_ref[hs, :] / stats[g][1]
        o_ref[:, hs] = out.T.astype(o_ref.dtype)


def _attention_unshifted_kernel(q_ref, k_ref, vt_ref, o_ref, acc_ref, s0_ref, *, tq, tk, n_kv_blocks):
    heads = [slice(g * HEAD_DIM, (g + 1) * HEAD_DIM) for g in range(GROUP)]
    n_trips = (q_ref.shape[0] // tq) * n_kv_blocks
    acc_ref[...] = jnp.zeros_like(acc_ref)

    def offsets(t):
        return pl.multiple_of((t // n_kv_blocks) * tq, tq), pl.multiple_of((t % n_kv_blocks) * tk, tk)

    def scores(t, g):
        qoff, koff = offsets(t)
        return lax.dot_general(k_ref[pl.ds(koff, tk), :], q_ref[pl.ds(qoff, tq), heads[g]], NT_DIMS,
                               preferred_element_type=F32)

    s0_ref[...] = scores(0, 0)

    def body(t, carry):
        qoff, koff = offsets(t)
        vb = vt_ref[:, pl.ds(koff, tk)]
        new = []
        s = s0_ref[...]
        for g in range(GROUP):
            if g + 1 < GROUP:
                s_next = scores(t, g + 1)
            else:
                s_next = scores(jnp.minimum(t + 1, n_trips - 1), 0)
            p = jnp.exp2(s)
            new.append(carry[g] + jnp.sum(p, axis=0, keepdims=True))
            acc_ref[heads[g], :] += jnp.dot(vb, p.astype(BF16), preferred_element_type=F32)
            s = s_next
        s0_ref[...] = s
        tile_done = (t % n_kv_blocks) == n_kv_blocks - 1

        @pl.when(tile_done)
        def _():
            for g in range(GROUP):
                out = acc_ref[heads[g], :] / new[g]
                o_ref[pl.ds(qoff, tq), heads[g]] = out.T.astype(o_ref.dtype)
            acc_ref[...] = jnp.zeros_like(acc_ref)

        return tuple(jnp.where(tile_done, 0.0, l) for l in new)

    init = tuple(jnp.zeros((1, tq), F32) for _ in range(GROUP))
    lax.fori_loop(0, n_trips, body, init)


def _attention(q, k, vt, n_seq, seq_len, body):
    t = k.shape[0]
    unshifted = body is _attention_unshifted_kernel
    tq = min(256, seq_len)
    gw = GROUP * HEAD_DIM
    scratch = [pltpu.VMEM((gw, tq), F32)]
    if unshifted:
        tk = min(8192, seq_len)
        bq = min(4096, seq_len)
        scratch.append(pltpu.VMEM((tk, tq), F32))
        kern = functools.partial(body, tq=tq, tk=tk, n_kv_blocks=seq_len // tk)
    else:
        tk = min(512, seq_len)
        bq = tq
        kern = functools.partial(body, tk=tk, n_kv_blocks=seq_len // tk)
    q_blocks = seq_len // bq
    return pl.pallas_call(
        kern,
        grid=(n_seq, N_KV, q_blocks),
        in_specs=[
            pl.BlockSpec((bq, gw), lambda b, h, i: (b * q_blocks + i, h)),
            pl.BlockSpec((seq_len, HEAD_DIM), lambda b, h, i: (b, h)),
            pl.BlockSpec((HEAD_DIM, seq_len), lambda b, h, i: (h, b)),
        ],
        out_specs=pl.BlockSpec((bq, gw), lambda b, h, i: (b * q_blocks + i, h)),
        out_shape=jax.ShapeDtypeStruct((t, D_ATTN), BF16),
        scratch_shapes=scratch,
        compiler_params=pltpu.CompilerParams(
            dimension_semantics=("parallel", "parallel", "arbitrary"), vmem_limit_bytes=VMEM_LIMIT),
        name="attention_unshifted" if unshifted else "attention_online",
    )(q, k, vt)


def _conv_kernel(a_ref, b_ref, z_ref, ap_ref, bp_ref, an_ref, bn_ref,
                 w_ref, bias_ref, lng_ref, lnb_ref, o_ref, ext_ref, sh_ref, cv_ref, *, tiles_per_seq):
    tm = a_ref.shape[0]
    pos = pl.program_id(0) % tiles_per_seq

    def glu(a, b):
        return a.astype(F32) * _sigmoid(b.astype(F32))

    ext_ref[0:HALO, :] = jnp.where(pos != 0, glu(ap_ref[...], bp_ref[...]), 0.0)
    ext_ref[HALO:HALO + tm, :] = glu(a_ref[...], b_ref[...])
    ext_ref[HALO + tm:2 * HALO + tm, :] = jnp.where(
        pos != tiles_per_seq - 1, glu(an_ref[...], bn_ref[...]), 0.0)

    span = tm + SHIFT_SPAN
    sh_ref[0] = ext_ref[0:span, :]
    for s in range(1, SUBLANES):
        sh_ref[s] = ext_ref[s:s + span, :]

    base = HALO - CONV_PAD
    rows = min(tm, CONV_ACC_ROWS)
    for r0 in range(0, tm, rows):
        for c in range(D_CONV // LANES):
            cs = slice(c * LANES, (c + 1) * LANES)
            acc = jnp.zeros((rows, LANES), F32)
            for k in range(CONV_K):
                s, a = (base + k) % SUBLANES, (base + k) // SUBLANES * SUBLANES
                acc = acc + sh_ref[s, r0 + a:r0 + a + rows, cs] * w_ref[k:k + 1, cs]
            cv_ref[r0:r0 + rows, cs] = acc + bias_ref[:, cs]

    u = cv_ref[...]
    mu = jnp.mean(u, axis=-1, keepdims=True)
    uc = u - mu
    var = jnp.mean(uc * uc, axis=-1, keepdims=True)
    y = uc * lax.rsqrt(var + EPS) * lng_ref[...] + lnb_ref[...]
    o_ref[...] = (_silu(y) * _silu(z_ref[...].astype(F32))).astype(o_ref.dtype)


def _conv_branch(proj, dw, bias, lng, lnb, seq_len):
    t = proj.shape[0]
    tm = min(256, seq_len)
    tiles_per_seq = seq_len // tm
    hb = tm // HALO
    n_hb = t // HALO
    ca, cb, cz = OFF_GLU_A // D_CONV, OFF_GLU_B // D_CONV, OFF_ZCONV // D_CONV
    prev_map = lambda c: (lambda i: (jnp.maximum(i * hb - 1, 0), c))
    next_map = lambda c: (lambda i: (jnp.minimum((i + 1) * hb, n_hb - 1), c))
    row = lambda i: (0, 0)
    return pl.pallas_call(
        functools.partial(_conv_kernel, tiles_per_seq=tiles_per_seq),
        grid=(t // tm,),
        in_specs=[
            pl.BlockSpec((tm, D_CONV), lambda i: (i, ca)),
            pl.BlockSpec((tm, D_CONV), lambda i: (i, cb)),
            pl.BlockSpec((tm, D_CONV), lambda i: (i, cz)),
            pl.BlockSpec((HALO, D_CONV), prev_map(ca)),
            pl.BlockSpec((HALO, D_CONV), prev_map(cb)),
            pl.BlockSpec((HALO, D_CONV), next_map(ca)),
            pl.BlockSpec((HALO, D_CONV), next_map(cb)),
            pl.BlockSpec((CONV_K, D_CONV), row),
            pl.BlockSpec((1, D_CONV), row),
            pl.BlockSpec((1, D_CONV), row),
            pl.BlockSpec((1, D_CONV), row),
        ],
        out_specs=pl.BlockSpec((tm, D_CONV), lambda i: (i, 0)),
        out_shape=jax.ShapeDtypeStruct((t, D_CONV), BF16),
        scratch_shapes=[pltpu.VMEM((tm + 2 * HALO, D_CONV), F32),
                        pltpu.VMEM((SUBLANES, tm + SHIFT_SPAN, D_CONV), F32),
                        pltpu.VMEM((tm, D_CONV), F32)],
        compiler_params=pltpu.CompilerParams(
            dimension_semantics=("parallel",), vmem_limit_bytes=VMEM_LIMIT),
        name="conv_branch",
    )(proj, proj, proj, proj, proj, proj, proj, dw, bias, lng, lnb)


def _merge_kernel(u_ref, a_ref, z_ref, gc_ref, ga_ref, x_ref, wc_ref, wa_ref, wo_ref, pg_ref, o_ref):
    y_conv = jnp.dot(u_ref[...], wc_ref[...], preferred_element_type=F32)
    ag = (a_ref[...].astype(F32) * _silu(z_ref[...].astype(F32))).astype(BF16)
    y_attn = jnp.dot(ag, wa_ref[...], preferred_element_type=F32)
    m = _sigmoid(gc_ref[...].astype(F32)) * y_conv + _sigmoid(ga_ref[...].astype(F32)) * y_attn
    out = jnp.dot(m.astype(BF16), wo_ref[...], preferred_element_type=F32)
    ms = jnp.mean(out * out, axis=-1, keepdims=True)
    o_ref[...] = x_ref[...] + out * lax.rsqrt(ms + EPS) * pg_ref[...]


def _merge_out(u, a, proj, x, wc, wa, wo, pg):
    t = x.shape[0]
    tm = min(256, t)
    const = lambda i: (0, 0)
    resident = functools.partial(pl.BlockSpec, index_map=const, pipeline_mode=pl.Buffered(1))
    return pl.pallas_call(
        _merge_kernel,
        grid=(t // tm,),
        in_specs=[
            pl.BlockSpec((tm, D_CONV), lambda i: (i, 0)),
            pl.BlockSpec((tm, D_ATTN), lambda i: (i, 0)),
            pl.BlockSpec((tm, D_ATTN), lambda i: (i, OFF_ZATTN // D_ATTN)),
            pl.BlockSpec((tm, D_MODEL), lambda i: (i, OFF_GCONV // D_MODEL)),
            pl.BlockSpec((tm, D_MODEL), lambda i: (i, OFF_GATTN // D_MODEL)),
            pl.BlockSpec((tm, D_MODEL), lambda i: (i, 0)),
            resident((D_CONV, D_MODEL)),
            resident((D_ATTN, D_MODEL)),
            resident((D_MODEL, D_MODEL)),
            pl.BlockSpec((1, D_MODEL), const),
        ],
        out_specs=pl.BlockSpec((tm, D_MODEL), lambda i: (i, 0)),
        out_shape=jax.ShapeDtypeStruct((t, D_MODEL), F32),
        compiler_params=pltpu.CompilerParams(
            dimension_semantics=("parallel",), vmem_limit_bytes=VMEM_LIMIT),
        name="merge_out",
    )(u, a, proj, proj, proj, x, wc, wa, wo, pg)


def _pair_major(w):
    lead = w.shape[:-1]
    n = w.shape[-1] // HEAD_DIM
    w = w.reshape(lead + (n, 2, 2, ROPE_FREQS))
    w = jnp.swapaxes(w, -2, -3)
    return w.reshape(lead + (n * HEAD_DIM,))


def _pack_w_in(w_in):
    o = [0, D_CONV, 2 * D_CONV, 3 * D_CONV]
    o.append(o[-1] + D_ATTN)
    o.append(o[-1] + D_KV)
    o.append(o[-1] + D_KV)
    o.append(o[-1] + D_ATTN)
    o.append(o[-1] + D_MODEL)
    o.append(o[-1] + D_MODEL)
    w_in = w_in.astype(BF16)
    glu_a, glu_b, z_conv, q, k, v, z_attn, g_conv, g_attn = (
        w_in[..., o[n]:o[n + 1]] for n in range(9))
    return jnp.concatenate(
        [_pair_major(q), z_attn, g_conv, g_attn, glu_a, glu_b, z_conv, _pair_major(k), v], axis=-1)


def _rope_tables(seq_len):
    rows = seq_len // GRID_W
    row_idx = jnp.broadcast_to(jnp.arange(rows, dtype=F32)[:, None], (rows, GRID_W)).reshape(-1)
    col_idx = jnp.broadcast_to(jnp.arange(GRID_W, dtype=F32)[None, :], (rows, GRID_W)).reshape(-1)
    inv_freq = ROPE_THETA ** (-(jnp.arange(ROPE_FREQS, dtype=F32) * 2.0) / ROPE_AXIS_DIM)
    ar = row_idx[:, None] * inv_freq
    ac = col_idx[:, None] * inv_freq
    cos = jnp.concatenate([jnp.cos(ar), jnp.cos(ac), jnp.cos(ar), jnp.cos(ac)], axis=-1)
    sin = jnp.concatenate([-jnp.sin(ar), -jnp.sin(ac), jnp.sin(ar), jnp.sin(ac)], axis=-1)
    return cos, sin


def _layer(x, n_seq, seq_len, tables, p):
    cos, sin = tables
    proj = _inproj(x, p["pre_g"], p["w_in"])
    q, k, vt = _qkv_prep(proj, cos, sin, p["gq"], p["gk"], seq_len)
    score_bound = HEAD_DIM * Q_SCALE * jnp.max(jnp.abs(p["gq"])) * jnp.max(jnp.abs(p["gk"]))
    a = lax.cond(
        score_bound <= MAX_UNSHIFTED,
        functools.partial(_attention, n_seq=n_seq, seq_len=seq_len, body=_attention_unshifted_kernel),
        functools.partial(_attention, n_seq=n_seq, seq_len=seq_len, body=_attention_kernel),
        q, k, vt)
    u = _conv_branch(proj, p["dw"], p["dw_b"], p["ln_g"], p["ln_b"], seq_len)
    return _merge_out(u, a, proj, x, p["w_conv_out"], p["w_attn_o"], p["w_out"], p["post_g"])


def kernel(x_prompt, x_sample, pre_norm_g, w_in, dw_kernel, dw_bias, conv_ln_g, conv_ln_b,
           w_conv_out, q_norm_g, k_norm_g, w_attn_o, w_out, post_norm_g):
    depth = w_in.shape[0]
    w_in_p = _pack_w_in(w_in)
    wc = w_conv_out.astype(BF16)
    wa = w_attn_o.astype(BF16)
    wo = w_out.astype(BF16)
    gq = _pair_major(q_norm_g)
    gk = _pair_major(k_norm_g)

    streams = []
    for x in (x_prompt, x_sample):
        b, s, d = x.shape
        streams.append((x.reshape(b * s, d), b, s, _rope_tables(s)))

    outs = []
    for x2, b, s, tables in streams:
        y = x2
        for l in range(depth):
            p = dict(
                pre_g=pre_norm_g[l][None, :], w_in=w_in_p[l], dw=dw_kernel[l, :, 0, :],
                dw_b=dw_bias[l][None, :], ln_g=conv_ln_g[l][None, :], ln_b=conv_ln_b[l][None, :],
                w_conv_out=wc[l], gq=gq[l][None, :], gk=gk[l][None, :], w_attn_o=wa[l],
                w_out=wo[l], post_g=post_norm_g[l][None, :])
            y = _layer(y, b, s, tables, p)
        outs.append(y.reshape(b, s, d))
    return tuple(outs)
```

```python
import functools
import math

import jax
import jax.numpy as jnp
from jax import lax
from jax.experimental import pallas as pl
from jax.experimental.pallas import tpu as pltpu

F32 = jnp.float32
BF16 = jnp.bfloat16

D_MODEL = 2048
GRID_W = 64
D_CONV = D_MODEL // 2
CONV_K = 31
CONV_PAD = CONV_K // 2
HEAD_DIM = 128
N_HEADS = D_MODEL // HEAD_DIM
N_KV = 4
GROUP = N_HEADS // N_KV
D_ATTN = N_HEADS * HEAD_DIM
D_KV = N_KV * HEAD_DIM
ROPE_AXIS_DIM = HEAD_DIM // 2
ROPE_FREQS = ROPE_AXIS_DIM // 2
ROPE_THETA = 10000.0
EPS = 1e-6
N_IN = 3 * D_CONV + 2 * D_ATTN + 2 * D_KV + 2 * D_MODEL

OFF_Q = 0
OFF_ZATTN = OFF_Q + D_ATTN
OFF_GCONV = OFF_ZATTN + D_ATTN
OFF_GATTN = OFF_GCONV + D_MODEL
OFF_GLU_A = OFF_GATTN + D_MODEL
OFF_GLU_B = OFF_GLU_A + D_CONV
OFF_ZCONV = OFF_GLU_B + D_CONV
OFF_K = OFF_ZCONV + D_CONV
OFF_V = OFF_K + D_KV

HALO = 16
LANES = 128
SUBLANES = 8
CONV_ACC_ROWS = 128
SHIFT_SPAN = (HALO - CONV_PAD + CONV_K - 1) // SUBLANES * SUBLANES
INPROJ_DOT_COLS = 1024
INPROJ_ROWS = 1024
PREP_ROWS = 512
ATTN_Q_TILE = 256
ATTN_KEY_BLOCK = 8192
ATTN_Q_ROWS_PER_STEP = 4096
ONLINE_KEY_BLOCK = 512
CONV_ROWS = 256
MERGE_ROWS = 256
VMEM_LIMIT = 56 * 1024 * 1024
NEG_BIG = -1e30
MAX_UNSHIFTED = 60.0
NT_DIMS = (((1,), (1,)), ((), ()))
Q_SCALE = (1.0 / math.sqrt(HEAD_DIM)) * math.log2(math.e)


def _sigmoid(x):
    return 1.0 / (1.0 + jnp.exp(-x))


def _silu(x):
    return x * _sigmoid(x)


def _inproj_kernel(x_ref, g_ref, w_ref, o_ref, h_ref):
    @pl.when(pl.program_id(1) == 0)
    def _():
        x = x_ref[...]
        ms = jnp.mean(x * x, axis=-1, keepdims=True)
        h_ref[...] = (x * lax.rsqrt(ms + EPS) * g_ref[...]).astype(BF16)

    for c in range(0, o_ref.shape[1], INPROJ_DOT_COLS):
        cs = slice(c, c + INPROJ_DOT_COLS)
        o_ref[:, cs] = jnp.dot(h_ref[...], w_ref[:, cs], preferred_element_type=F32).astype(o_ref.dtype)


def _inproj(x, g, w):
    t = x.shape[0]
    tm = min(INPROJ_ROWS, t)
    tn = 2 * INPROJ_DOT_COLS
    return pl.pallas_call(
        _inproj_kernel,
        grid=(t // tm, N_IN // tn),
        in_specs=[
            pl.BlockSpec((tm, D_MODEL), lambda i, j: (i, 0)),
            pl.BlockSpec((1, D_MODEL), lambda i, j: (0, 0)),
            pl.BlockSpec((D_MODEL, tn), lambda i, j: (0, j)),
        ],
        out_specs=pl.BlockSpec((tm, tn), lambda i, j: (i, j)),
        out_shape=jax.ShapeDtypeStruct((t, N_IN), BF16),
        scratch_shapes=[pltpu.VMEM((tm, D_MODEL), BF16)],
        compiler_params=pltpu.CompilerParams(
            dimension_semantics=("parallel", "arbitrary"), vmem_limit_bytes=VMEM_LIMIT),
        name="inproj",
    )(x, g, w)


def _norm_rope(x, g, cos, sin):
    ones = jnp.ones((HEAD_DIM, HEAD_DIM), F32)
    ms = jnp.dot(x * x, ones, preferred_element_type=F32) * (1.0 / HEAD_DIM)
    y = x * lax.rsqrt(ms + EPS) * g
    return y * cos + pltpu.roll(y, HEAD_DIM // 2, axis=1) * sin


def _qkv_prep_kernel(q_ref, k_ref, v_ref, cos_ref, sin_ref, gq_ref, gk_ref,
                     qo_ref, ko_ref, vt_ref):
    cos = cos_ref[...]
    sin = sin_ref[...]
    gq = gq_ref[...] * Q_SCALE
    gk = gk_ref[...]
    for h in range(N_HEADS):
        hs = slice(h * HEAD_DIM, (h + 1) * HEAD_DIM)
        y = _norm_rope(q_ref[:, hs].astype(F32), gq, cos, sin)
        qo_ref[:, hs] = y.astype(BF16)
    for h in range(N_KV):
        hs = slice(h * HEAD_DIM, (h + 1) * HEAD_DIM)
        y = _norm_rope(k_ref[:, hs].astype(F32), gk, cos, sin)
        ko_ref[:, hs] = y.astype(BF16)
        vt_ref[hs, :] = v_ref[:, hs].astype(F32).T.astype(BF16)


def _qkv_prep(proj, cos, sin, gq, gk, seq_len):
    t = proj.shape[0]
    tm = min(PREP_ROWS, seq_len)
    tiles_per_seq = seq_len // tm
    return pl.pallas_call(
        _qkv_prep_kernel,
        grid=(t // tm,),
        in_specs=[
            pl.BlockSpec((tm, D_ATTN), lambda i: (i, OFF_Q // D_ATTN)),
            pl.BlockSpec((tm, D_KV), lambda i: (i, OFF_K // D_KV)),
            pl.BlockSpec((tm, D_KV), lambda i: (i, OFF_V // D_KV)),
            pl.BlockSpec((tm, HEAD_DIM), lambda i: (i % tiles_per_seq, 0)),
            pl.BlockSpec((tm, HEAD_DIM), lambda i: (i % tiles_per_seq, 0)),
            pl.BlockSpec((1, HEAD_DIM), lambda i: (0, 0)),
            pl.BlockSpec((1, HEAD_DIM), lambda i: (0, 0)),
        ],
        out_specs=[
            pl.BlockSpec((tm, D_ATTN), lambda i: (i, 0)),
            pl.BlockSpec((tm, D_KV), lambda i: (i, 0)),
            pl.BlockSpec((D_KV, tm), lambda i: (0, i)),
        ],
        out_shape=[
            jax.ShapeDtypeStruct((t, D_ATTN), BF16),
            jax.ShapeDtypeStruct((t, D_KV), BF16),
            jax.ShapeDtypeStruct((D_KV, t), BF16),
        ],
        compiler_params=pltpu.CompilerParams(
            dimension_semantics=("parallel",), vmem_limit_bytes=VMEM_LIMIT),
        name="qkv_prep",
    )(proj, proj, proj, cos, sin, gq, gk)


def _attention_kernel(q_ref, k_ref, vt_ref, o_ref, acc_ref, *, tk, n_kv_blocks):
    tq = q_ref.shape[0]
    acc_ref[...] = jnp.zeros_like(acc_ref)

    def body(j, carry):
        off = pl.multiple_of(j * tk, tk)
        kb = k_ref[pl.ds(off, tk), :]
        vb = vt_ref[:, pl.ds(off, tk)]
        new = []
        for g in range(GROUP):
            m, l = carry[g]
            hs = slice(g * HEAD_DIM, (g + 1) * HEAD_DIM)
            s = lax.dot_general(kb, q_ref[:, hs], NT_DIMS, preferred_element_type=F32)
            m_new = jnp.maximum(m, jnp.max(s, axis=0, keepdims=True))
            alpha = jnp.exp2(m - m_new)
            p = jnp.exp2(s - m_new)
            l_new = alpha * l + jnp.sum(p, axis=0, keepdims=True)
            pv = jnp.dot(vb, p.astype(BF16), preferred_element_type=F32)
            acc_ref[hs, :] = alpha * acc_ref[hs, :] + pv
            new.append((m_new, l_new))
        return tuple(new)

    init = tuple((jnp.full((1, tq), NEG_BIG, F32), jnp.zeros((1, tq), F32)) for _ in range(GROUP))
    stats = lax.fori_loop(0, n_kv_blocks, body, init)
    for g in range(GROUP):
        hs = slice(g * HEAD_DIM, (g + 1) * HEAD_DIM)
        out = acc_ref[hs, :] / stats[g][1]
        o_ref[:, hs] = out.T.astype(o_ref.dtype)


def _attention_unshifted_kernel(q_ref, k_ref, vt_ref, o_ref, acc_ref, s0_ref, *, tq, tk, n_kv_blocks):
    heads = [slice(g * HEAD_DIM, (g + 1) * HEAD_DIM) for g in range(GROUP)]
    n_trips = (q_ref.shape[0] // tq) * n_kv_blocks
    acc_ref[...] = jnp.zeros_like(acc_ref)

    def offsets(t):
        return pl.multiple_of((t // n_kv_blocks) * tq, tq), pl.multiple_of((t % n_kv_blocks) * tk, tk)

    def scores(t, g):
        qoff, koff = offsets(t)
        return lax.dot_general(k_ref[pl.ds(koff, tk), :], q_ref[pl.ds(qoff, tq), heads[g]], NT_DIMS,
                               preferred_element_type=F32)

    s0_ref[...] = scores(0, 0)

    def body(t, carry):
        qoff, koff = offsets(t)
        vb = vt_ref[:, pl.ds(koff, tk)]
        new = []
        s = s0_ref[...]
        for g in range(GROUP):
            if g + 1 < GROUP:
                s_next = scores(t, g + 1)
            else:
                s_next = scores(jnp.minimum(t + 1, n_trips - 1), 0)
            p = jnp.exp2(s)
            new.append(carry[g] + jnp.sum(p, axis=0, keepdims=True))
            acc_ref[heads[g], :] += jnp.dot(vb, p.astype(BF16), preferred_element_type=F32)
            s = s_next
        s0_ref[...] = s
        tile_done = (t % n_kv_blocks) == n_kv_blocks - 1

        @pl.when(tile_done)
        def _():
            for g in range(GROUP):
                out = acc_ref[heads[g], :] / new[g]
                o_ref[pl.ds(qoff, tq), heads[g]] = out.T.astype(o_ref.dtype)
            acc_ref[...] = jnp.zeros_like(acc_ref)

        return tuple(jnp.where(tile_done, 0.0, l) for l in new)

    init = tuple(jnp.zeros((1, tq), F32) for _ in range(GROUP))
    lax.fori_loop(0, n_trips, body, init)


def _attention(q, k, vt, n_seq, seq_len, body):
    t = k.shape[0]
    unshifted = body is _attention_unshifted_kernel
    tq = min(ATTN_Q_TILE, seq_len)
    gw = GROUP * HEAD_DIM
    scratch = [pltpu.VMEM((gw, tq), F32)]
    if unshifted:
        tk = min(ATTN_KEY_BLOCK, seq_len)
        bq = min(ATTN_Q_ROWS_PER_STEP, seq_len)
        scratch.append(pltpu.VMEM((tk, tq), F32))
        kern = functools.partial(body, tq=tq, tk=tk, n_kv_blocks=seq_len // tk)
    else:
        tk = min(ONLINE_KEY_BLOCK, seq_len)
        bq = tq
        kern = functools.partial(body, tk=tk, n_kv_blocks=seq_len // tk)
    q_blocks = seq_len // bq
    return pl.pallas_call(
        kern,
        grid=(n_seq, N_KV, q_blocks),
        in_specs=[
            pl.BlockSpec((bq, gw), lambda b, h, i: (b * q_blocks + i, h)),
            pl.BlockSpec((seq_len, HEAD_DIM), lambda b, h, i: (b, h)),
            pl.BlockSpec((HEAD_DIM, seq_len), lambda b, h, i: (h, b)),
        ],
        out_specs=pl.BlockSpec((bq, gw), lambda b, h, i: (b * q_blocks + i, h)),
        out_shape=jax.ShapeDtypeStruct((t, D_ATTN), BF16),
        scratch_shapes=scratch,
        compiler_params=pltpu.CompilerParams(
            dimension_semantics=("parallel", "parallel", "arbitrary"), vmem_limit_bytes=VMEM_LIMIT),
        name="attention_unshifted" if unshifted else "attention_online",
    )(q, k, vt)


def _conv_kernel(a_ref, b_ref, z_ref, ap_ref, bp_ref, an_ref, bn_ref,
                 w_ref, bias_ref, lng_ref, lnb_ref, o_ref, ext_ref, sh_ref, cv_ref, *, tiles_per_seq):
    tm = a_ref.shape[0]
    pos = pl.program_id(0) % tiles_per_seq

    def glu(a, b):
        return a.astype(F32) * _sigmoid(b.astype(F32))

    ext_ref[0:HALO, :] = jnp.where(pos != 0, glu(ap_ref[...], bp_ref[...]), 0.0)
    ext_ref[HALO:HALO + tm, :] = glu(a_ref[...], b_ref[...])
    ext_ref[HALO + tm:2 * HALO + tm, :] = jnp.where(
        pos != tiles_per_seq - 1, glu(an_ref[...], bn_ref[...]), 0.0)

    span = tm + SHIFT_SPAN
    sh_ref[0] = ext_ref[0:span, :]
    for s in range(1, SUBLANES):
        sh_ref[s] = ext_ref[s:s + span, :]

    base = HALO - CONV_PAD
    rows = min(tm, CONV_ACC_ROWS)
    for r0 in range(0, tm, rows):
        for c in range(D_CONV // LANES):
            cs = slice(c * LANES, (c + 1) * LANES)
            acc = jnp.zeros((rows, LANES), F32)
            for k in range(CONV_K):
                s, a = (base + k) % SUBLANES, (base + k) // SUBLANES * SUBLANES
                acc = acc + sh_ref[s, r0 + a:r0 + a + rows, cs] * w_ref[k:k + 1, cs]
            cv_ref[r0:r0 + rows, cs] = acc + bias_ref[:, cs]

    u = cv_ref[...]
    mu = jnp.mean(u, axis=-1, keepdims=True)
    uc = u - mu
    var = jnp.mean(uc * uc, axis=-1, keepdims=True)
    y = uc * lax.rsqrt(var + EPS) * lng_ref[...] + lnb_ref[...]
    o_ref[...] = (_silu(y) * _silu(z_ref[...].astype(F32))).astype(o_ref.dtype)


def _conv_branch(proj, dw, bias, lng, lnb, seq_len):
    t = proj.shape[0]
    tm = min(CONV_ROWS, seq_len)
    tiles_per_seq = seq_len // tm
    hb = tm // HALO
    n_hb = t // HALO
    ca, cb, cz = OFF_GLU_A // D_CONV, OFF_GLU_B // D_CONV, OFF_ZCONV // D_CONV
    prev_map = lambda c: (lambda i: (jnp.maximum(i * hb - 1, 0), c))
    next_map = lambda c: (lambda i: (jnp.minimum((i + 1) * hb, n_hb - 1), c))
    row = lambda i: (0, 0)
    return pl.pallas_call(
        functools.partial(_conv_kernel, tiles_per_seq=tiles_per_seq),
        grid=(t // tm,),
        in_specs=[
            pl.BlockSpec((tm, D_CONV), lambda i: (i, ca)),
            pl.BlockSpec((tm, D_CONV), lambda i: (i, cb)),
            pl.BlockSpec((tm, D_CONV), lambda i: (i, cz)),
            pl.BlockSpec((HALO, D_CONV), prev_map(ca)),
            pl.BlockSpec((HALO, D_CONV), prev_map(cb)),
            pl.BlockSpec((HALO, D_CONV), next_map(ca)),
            pl.BlockSpec((HALO, D_CONV), next_map(cb)),
            pl.BlockSpec((CONV_K, D_CONV), row),
            pl.BlockSpec((1, D_CONV), row),
            pl.BlockSpec((1, D_CONV), row),
            pl.BlockSpec((1, D_CONV), row),
        ],
        out_specs=pl.BlockSpec((tm, D_CONV), lambda i: (i, 0)),
        out_shape=jax.ShapeDtypeStruct((t, D_CONV), BF16),
        scratch_shapes=[pltpu.VMEM((tm + 2 * HALO, D_CONV), F32),
                        pltpu.VMEM((SUBLANES, tm + SHIFT_SPAN, D_CONV), F32),
                        pltpu.VMEM((tm, D_CONV), F32)],
        compiler_params=pltpu.CompilerParams(
            dimension_semantics=("parallel",), vmem_limit_bytes=VMEM_LIMIT),
        name="conv_branch",
    )(proj, proj, proj, proj, proj, proj, proj, dw, bias, lng, lnb)


def _merge_kernel(u_ref, a_ref, z_ref, gc_ref, ga_ref, x_ref, wc_ref, wa_ref, wo_ref, pg_ref, o_ref):
    y_conv = jnp.dot(u_ref[...], wc_ref[...], preferred_element_type=F32)
    ag = (a_ref[...].astype(F32) * _silu(z_ref[...].astype(F32))).astype(BF16)
    y_attn = jnp.dot(ag, wa_ref[...], preferred_element_type=F32)
    m = _sigmoid(gc_ref[...].astype(F32)) * y_conv + _sigmoid(ga_ref[...].astype(F32)) * y_attn
    out = jnp.dot(m.astype(BF16), wo_ref[...], preferred_element_type=F32)
    ms = jnp.mean(out * out, axis=-1, keepdims=True)
    o_ref[...] = x_ref[...] + out * lax.rsqrt(ms + EPS) * pg_ref[...]


def _merge_out(u, a, proj, x, wc, wa, wo, pg):
    t = x.shape[0]
    tm = min(MERGE_ROWS, t)
    const = lambda i: (0, 0)
    resident = functools.partial(pl.BlockSpec, index_map=const, pipeline_mode=pl.Buffered(1))
    return pl.pallas_call(
        _merge_kernel,
        grid=(t // tm,),
        in_specs=[
            pl.BlockSpec((tm, D_CONV), lambda i: (i, 0)),
            pl.BlockSpec((tm, D_ATTN), lambda i: (i, 0)),
            pl.BlockSpec((tm, D_ATTN), lambda i: (i, OFF_ZATTN // D_ATTN)),
            pl.BlockSpec((tm, D_MODEL), lambda i: (i, OFF_GCONV // D_MODEL)),
            pl.BlockSpec((tm, D_MODEL), lambda i: (i, OFF_GATTN // D_MODEL)),
            pl.BlockSpec((tm, D_MODEL), lambda i: (i, 0)),
            resident((D_CONV, D_MODEL)),
            resident((D_ATTN, D_MODEL)),
            resident((D_MODEL, D_MODEL)),
            pl.BlockSpec((1, D_MODEL), const),
        ],
        out_specs=pl.BlockSpec((tm, D_MODEL), lambda i: (i, 0)),
        out_shape=jax.ShapeDtypeStruct((t, D_MODEL), F32),
        compiler_params=pltpu.CompilerParams(
            dimension_semantics=("parallel",), vmem_limit_bytes=VMEM_LIMIT),
        name="merge_out",
    )(u, a, proj, proj, proj, x, wc, wa, wo, pg)


def _pair_major(w):
    lead = w.shape[:-1]
    n = w.shape[-1] // HEAD_DIM
    w = w.reshape(lead + (n, 2, 2, ROPE_FREQS))
    w = jnp.swapaxes(w, -2, -3)
    return w.reshape(lead + (n * HEAD_DIM,))


def _pack_w_in(w_in):
    o = [0, D_CONV, 2 * D_CONV, 3 * D_CONV]
    o.append(o[-1] + D_ATTN)
    o.append(o[-1] + D_KV)
    o.append(o[-1] + D_KV)
    o.append(o[-1] + D_ATTN)
    o.append(o[-1] + D_MODEL)
    o.append(o[-1] + D_MODEL)
    w_in = w_in.astype(BF16)
    glu_a, glu_b, z_conv, q, k, v, z_attn, g_conv, g_attn = (
        w_in[..., o[n]:o[n + 1]] for n in range(9))
    return jnp.concatenate(
        [_pair_major(q), z_attn, g_conv, g_attn, glu_a, glu_b, z_conv, _pair_major(k), v], axis=-1)


def _rope_tables(seq_len):
    rows = seq_len // GRID_W
    row_idx = jnp.broadcast_to(jnp.arange(rows, dtype=F32)[:, None], (rows, GRID_W)).reshape(-1)
    col_idx = jnp.broadcast_to(jnp.arange(GRID_W, dtype=F32)[None, :], (rows, GRID_W)).reshape(-1)
    inv_freq = ROPE_THETA ** (-(jnp.arange(ROPE_FREQS, dtype=F32) * 2.0) / ROPE_AXIS_DIM)
    ar = row_idx[:, None] * inv_freq
    ac = col_idx[:, None] * inv_freq
    cos = jnp.concatenate([jnp.cos(ar), jnp.cos(ac), jnp.cos(ar), jnp.cos(ac)], axis=-1)
    sin = jnp.concatenate([-jnp.sin(ar), -jnp.sin(ac), jnp.sin(ar), jnp.sin(ac)], axis=-1)
    return cos, sin


def _layer(x, n_seq, seq_len, tables, p):
    cos, sin = tables
    proj = _inproj(x, p["pre_g"], p["w_in"])
    q, k, vt = _qkv_prep(proj, cos, sin, p["gq"], p["gk"], seq_len)
    score_bound = HEAD_DIM * Q_SCALE * jnp.max(jnp.abs(p["gq"])) * jnp.max(jnp.abs(p["gk"]))
    a = lax.cond(
        score_bound <= MAX_UNSHIFTED,
        functools.partial(_attention, n_seq=n_seq, seq_len=seq_len, body=_attention_unshifted_kernel),
        functools.partial(_attention, n_seq=n_seq, seq_len=seq_len, body=_attention_kernel),
        q, k, vt)
    u = _conv_branch(proj, p["dw"], p["dw_b"], p["ln_g"], p["ln_b"], seq_len)
    return _merge_out(u, a, proj, x, p["w_conv_out"], p["w_attn_o"], p["w_out"], p["post_g"])


def kernel(x_prompt, x_sample, pre_norm_g, w_in, dw_kernel, dw_bias, conv_ln_g, conv_ln_b,
           w_conv_out, q_norm_g, k_norm_g, w_attn_o, w_out, post_norm_g):
    depth = w_in.shape[0]
    w_in_p = _pack_w_in(w_in)
    wc = w_conv_out.astype(BF16)
    wa = w_attn_o.astype(BF16)
    wo = w_out.astype(BF16)
    gq = _pair_major(q_norm_g)
    gk = _pair_major(k_norm_g)

    streams = []
    for x in (x_prompt, x_sample):
        b, s, d = x.shape
        streams.append((x.reshape(b * s, d), b, s, _rope_tables(s)))

    outs = []
    for x2, b, s, tables in streams:
        y = x2
        for l in range(depth):
            p = dict(
                pre_g=pre_norm_g[l][None, :], w_in=w_in_p[l], dw=dw_kernel[l, :, 0, :],
                dw_b=dw_bias[l][None, :], ln_g=conv_ln_g[l][None, :], ln_b=conv_ln_b[l][None, :],
                w_conv_out=wc[l], gq=gq[l][None, :], gk=gk[l][None, :], w_attn_o=wa[l],
                w_out=wo[l], post_g=post_norm_g[l][None, :])
            y = _layer(y, b, s, tables, p)
        outs.append(y.reshape(b, s, d))
    return tuple(outs)
```

```python
import functools
import math

import jax
import jax.numpy as jnp
from jax import lax
from jax.experimental import pallas as pl
from jax.experimental.pallas import tpu as pltpu

F32 = jnp.float32
BF16 = jnp.bfloat16

D_MODEL = 2048
GRID_W = 64
D_CONV = D_MODEL // 2
CONV_K = 31
CONV_PAD = CONV_K // 2
HEAD_DIM = 128
N_HEADS = D_MODEL // HEAD_DIM
N_KV = 4
GROUP = N_HEADS // N_KV
D_ATTN = N_HEADS * HEAD_DIM
D_KV = N_KV * HEAD_DIM
ROPE_AXIS_DIM = HEAD_DIM // 2
ROPE_FREQS = ROPE_AXIS_DIM // 2
ROPE_THETA = 10000.0
EPS = 1e-6
N_IN = 3 * D_CONV + 2 * D_ATTN + 2 * D_KV + 2 * D_MODEL

OFF_Q = 0
OFF_ZATTN = OFF_Q + D_ATTN
OFF_GCONV = OFF_ZATTN + D_ATTN
OFF_GATTN = OFF_GCONV + D_MODEL
OFF_GLU_A = OFF_GATTN + D_MODEL
OFF_GLU_B = OFF_GLU_A + D_CONV
OFF_ZCONV = OFF_GLU_B + D_CONV
OFF_K = OFF_ZCONV + D_CONV
OFF_V = OFF_K + D_KV

HALO = 16
LANES = 128
SUBLANES = 8
CONV_ACC_ROWS = 128
SHIFT_SPAN = (HALO - CONV_PAD + CONV_K - 1) // SUBLANES * SUBLANES
INPROJ_DOT_COLS = 1024
INPROJ_ROWS = 1024
PREP_ROWS = 1024
ATTN_Q_TILE = 256
ATTN_KEY_BLOCK = 8192
ATTN_Q_ROWS_PER_STEP = 4096
ONLINE_KEY_BLOCK = 512
CONV_ROWS = 512
MERGE_ROWS = 256
VMEM_LIMIT = 56 * 1024 * 1024
NEG_BIG = -1e30
MAX_UNSHIFTED = 60.0
NT_DIMS = (((1,), (1,)), ((), ()))
Q_SCALE = (1.0 / math.sqrt(HEAD_DIM)) * math.log2(math.e)


def _sigmoid(x):
    return 1.0 / (1.0 + jnp.exp(-x))


def _silu(x):
    return x * _sigmoid(x)


def _inproj_kernel(x_ref, g_ref, w_ref, o_ref, h_ref):
    @pl.when(pl.program_id(1) == 0)
    def _():
        x = x_ref[...]
        ms = jnp.mean(x * x, axis=-1, keepdims=True)
        h_ref[...] = (x * lax.rsqrt(ms + EPS) * g_ref[...]).astype(BF16)

    for c in range(0, o_ref.shape[1], INPROJ_DOT_COLS):
        cs = slice(c, c + INPROJ_DOT_COLS)
        o_ref[:, cs] = jnp.dot(h_ref[...], w_ref[:, cs], preferred_element_type=F32).astype(o_ref.dtype)


def _inproj(x, g, w):
    t = x.shape[0]
    tm = min(INPROJ_ROWS, t)
    tn = 2 * INPROJ_DOT_COLS
    return pl.pallas_call(
        _inproj_kernel,
        grid=(t // tm, N_IN // tn),
        in_specs=[
            pl.BlockSpec((tm, D_MODEL), lambda i, j: (i, 0)),
            pl.BlockSpec((1, D_MODEL), lambda i, j: (0, 0)),
            pl.BlockSpec((D_MODEL, tn), lambda i, j: (0, j)),
        ],
        out_specs=pl.BlockSpec((tm, tn), lambda i, j: (i, j)),
        out_shape=jax.ShapeDtypeStruct((t, N_IN), BF16),
        scratch_shapes=[pltpu.VMEM((tm, D_MODEL), BF16)],
        compiler_params=pltpu.CompilerParams(
            dimension_semantics=("parallel", "arbitrary"), vmem_limit_bytes=VMEM_LIMIT),
        name="inproj",
    )(x, g, w)


def _norm_rope(x, g, cos, sin):
    ones = jnp.ones((HEAD_DIM, HEAD_DIM), F32)
    ms = jnp.dot(x * x, ones, preferred_element_type=F32) * (1.0 / HEAD_DIM)
    y = x * lax.rsqrt(ms + EPS) * g
    return y * cos + pltpu.roll(y, HEAD_DIM // 2, axis=1) * sin


def _qkv_prep_kernel(q_ref, k_ref, v_ref, cos_ref, sin_ref, gq_ref, gk_ref,
                     qo_ref, ko_ref, vt_ref):
    cos = cos_ref[...]
    sin = sin_ref[...]
    gq = gq_ref[...] * Q_SCALE
    gk = gk_ref[...]
    for h in range(N_HEADS):
        hs = slice(h * HEAD_DIM, (h + 1) * HEAD_DIM)
        y = _norm_rope(q_ref[:, hs].astype(F32), gq, cos, sin)
        qo_ref[:, hs] = y.astype(BF16)
    for h in range(N_KV):
        hs = slice(h * HEAD_DIM, (h + 1) * HEAD_DIM)
        y = _norm_rope(k_ref[:, hs].astype(F32), gk, cos, sin)
        ko_ref[:, hs] = y.astype(BF16)
        vt_ref[hs, :] = v_ref[:, hs].astype(F32).T.astype(BF16)


def _qkv_prep(proj, cos, sin, gq, gk, seq_len):
    t = proj.shape[0]
    tm = min(PREP_ROWS, seq_len)
    tiles_per_seq = seq_len // tm
    return pl.pallas_call(
        _qkv_prep_kernel,
        grid=(t // tm,),
        in_specs=[
            pl.BlockSpec((tm, D_ATTN), lambda i: (i, OFF_Q // D_ATTN)),
            pl.BlockSpec((tm, D_KV), lambda i: (i, OFF_K // D_KV)),
            pl.BlockSpec((tm, D_KV), lambda i: (i, OFF_V // D_KV)),
            pl.BlockSpec((tm, HEAD_DIM), lambda i: (i % tiles_per_seq, 0)),
            pl.BlockSpec((tm, HEAD_DIM), lambda i: (i % tiles_per_seq, 0)),
            pl.BlockSpec((1, HEAD_DIM), lambda i: (0, 0)),
            pl.BlockSpec((1, HEAD_DIM), lambda i: (0, 0)),
        ],
        out_specs=[
            pl.BlockSpec((tm, D_ATTN), lambda i: (i, 0)),
            pl.BlockSpec((tm, D_KV), lambda i: (i, 0)),
            pl.BlockSpec((D_KV, tm), lambda i: (0, i)),
        ],
        out_shape=[
            jax.ShapeDtypeStruct((t, D_ATTN), BF16),
            jax.ShapeDtypeStruct((t, D_KV), BF16),
            jax.ShapeDtypeStruct((D_KV, t), BF16),
        ],
        compiler_params=pltpu.CompilerParams(
            dimension_semantics=("parallel",), vmem_limit_bytes=VMEM_LIMIT),
        name="qkv_prep",
    )(proj, proj, proj, cos, sin, gq, gk)


def _attention_kernel(q_ref, k_ref, vt_ref, o_ref, acc_ref, *, tk, n_kv_blocks):
    tq = q_ref.shape[0]
    acc_ref[...] = jnp.zeros_like(acc_ref)

    def body(j, carry):
        off = pl.multiple_of(j * tk, tk)
        kb = k_ref[pl.ds(off, tk), :]
        vb = vt_ref[:, pl.ds(off, tk)]
        new = []
        for g in range(GROUP):
            m, l = carry[g]
            hs = slice(g * HEAD_DIM, (g + 1) * HEAD_DIM)
            s = lax.dot_general(kb, q_ref[:, hs], NT_DIMS, preferred_element_type=F32)
            m_new = jnp.maximum(m, jnp.max(s, axis=0, keepdims=True))
            alpha = jnp.exp2(m - m_new)
            p = jnp.exp2(s - m_new)
            l_new = alpha * l + jnp.sum(p, axis=0, keepdims=True)
            pv = jnp.dot(vb, p.astype(BF16), preferred_element_type=F32)
            acc_ref[hs, :] = alpha * acc_ref[hs, :] + pv
            new.append((m_new, l_new))
        return tuple(new)

    init = tuple((jnp.full((1, tq), NEG_BIG, F32), jnp.zeros((1, tq), F32)) for _ in range(GROUP))
    stats = lax.fori_loop(0, n_kv_blocks, body, init)
    for g in range(GROUP):
        hs = slice(g * HEAD_DIM, (g + 1) * HEAD_DIM)
        out = acc_ref[hs, :] / stats[g][1]
        o_ref[:, hs] = out.T.astype(o_ref.dtype)


def _attention_unshifted_kernel(q_ref, k_ref, vt_ref, o_ref, acc_ref, s0_ref, *, tq, tk, n_kv_blocks):
    heads = [slice(g * HEAD_DIM, (g + 1) * HEAD_DIM) for g in range(GROUP)]
    n_trips = (q_ref.shape[0] // tq) * n_kv_blocks
    acc_ref[...] = jnp.zeros_like(acc_ref)

    def offsets(t):
        return pl.multiple_of((t // n_kv_blocks) * tq, tq), pl.multiple_of((t % n_kv_blocks) * tk, tk)

    def scores(t, g):
        qoff, koff = offsets(t)
        return lax.dot_general(k_ref[pl.ds(koff, tk), :], q_ref[pl.ds(qoff, tq), heads[g]], NT_DIMS,
                               preferred_element_type=F32)

    s0_ref[...] = scores(0, 0)

    def body(t, carry):
        qoff, koff = offsets(t)
        vb = vt_ref[:, pl.ds(koff, tk)]
        new = []
        s = s0_ref[...]
        for g in range(GROUP):
            if g + 1 < GROUP:
                s_next = scores(t, g + 1)
            else:
                s_next = scores(jnp.minimum(t + 1, n_trips - 1), 0)
            p = jnp.exp2(s)
            new.append(carry[g] + jnp.sum(p, axis=0, keepdims=True))
            acc_ref[heads[g], :] += jnp.dot(vb, p.astype(BF16), preferred_element_type=F32)
            s = s_next
        s0_ref[...] = s
        tile_done = (t % n_kv_blocks) == n_kv_blocks - 1

        @pl.when(tile_done)
        def _():
            for g in range(GROUP):
                out = acc_ref[heads[g], :] / new[g]
                o_ref[pl.ds(qoff, tq), heads[g]] = out.T.astype(o_ref.dtype)
            acc_ref[...] = jnp.zeros_like(acc_ref)

        return tuple(jnp.where(tile_done, 0.0, l) for l in new)

    init = tuple(jnp.zeros((1, tq), F32) for _ in range(GROUP))
    lax.fori_loop(0, n_trips, body, init)


def _attention(q, k, vt, n_seq, seq_len, body):
    t = k.shape[0]
    unshifted = body is _attention_unshifted_kernel
    tq = min(ATTN_Q_TILE, seq_len)
    gw = GROUP * HEAD_DIM
    scratch = [pltpu.VMEM((gw, tq), F32)]
    if unshifted:
        tk = min(ATTN_KEY_BLOCK, seq_len)
        bq = min(ATTN_Q_ROWS_PER_STEP, seq_len)
        scratch.append(pltpu.VMEM((tk, tq), F32))
        kern = functools.partial(body, tq=tq, tk=tk, n_kv_blocks=seq_len // tk)
    else:
        tk = min(ONLINE_KEY_BLOCK, seq_len)
        bq = tq
        kern = functools.partial(body, tk=tk, n_kv_blocks=seq_len // tk)
    q_blocks = seq_len // bq
    return pl.pallas_call(
        kern,
        grid=(n_seq, N_KV, q_blocks),
        in_specs=[
            pl.BlockSpec((bq, gw), lambda b, h, i: (b * q_blocks + i, h)),
            pl.BlockSpec((seq_len, HEAD_DIM), lambda b, h, i: (b, h)),
            pl.BlockSpec((HEAD_DIM, seq_len), lambda b, h, i: (h, b)),
        ],
        out_specs=pl.BlockSpec((bq, gw), lambda b, h, i: (b * q_blocks + i, h)),
        out_shape=jax.ShapeDtypeStruct((t, D_ATTN), BF16),
        scratch_shapes=scratch,
        compiler_params=pltpu.CompilerParams(
            dimension_semantics=("parallel", "parallel", "arbitrary"), vmem_limit_bytes=VMEM_LIMIT),
        name="attention_unshifted" if unshifted else "attention_online",
    )(q, k, vt)


def _conv_kernel(a_ref, b_ref, z_ref, ap_ref, bp_ref, an_ref, bn_ref,
                 w_ref, bias_ref, lng_ref, lnb_ref, o_ref, ext_ref, sh_ref, cv_ref, *, tiles_per_seq):
    tm = a_ref.shape[0]
    pos = pl.program_id(0) % tiles_per_seq

    def glu(a, b):
        return a.astype(F32) * _sigmoid(b.astype(F32))

    ext_ref[0:HALO, :] = jnp.where(pos != 0, glu(ap_ref[...], bp_ref[...]), 0.0)
    ext_ref[HALO:HALO + tm, :] = glu(a_ref[...], b_ref[...])
    ext_ref[HALO + tm:2 * HALO + tm, :] = jnp.where(
        pos != tiles_per_seq - 1, glu(an_ref[...], bn_ref[...]), 0.0)

    span = tm + SHIFT_SPAN
    sh_ref[0] = ext_ref[0:span, :]
    for s in range(1, SUBLANES):
        sh_ref[s] = ext_ref[s:s + span, :]

    base = HALO - CONV_PAD
    rows = min(tm, CONV_ACC_ROWS)
    for r0 in range(0, tm, rows):
        for c in range(D_CONV // LANES):
            cs = slice(c * LANES, (c + 1) * LANES)
            acc = jnp.zeros((rows, LANES), F32)
            for k in range(CONV_K):
                s, a = (base + k) % SUBLANES, (base + k) // SUBLANES * SUBLANES
                acc = acc + sh_ref[s, r0 + a:r0 + a + rows, cs] * w_ref[k:k + 1, cs]
            cv_ref[r0:r0 + rows, cs] = acc + bias_ref[:, cs]

    u = cv_ref[...]
    mu = jnp.mean(u, axis=-1, keepdims=True)
    uc = u - mu
    var = jnp.mean(uc * uc, axis=-1, keepdims=True)
    y = uc * lax.rsqrt(var + EPS) * lng_ref[...] + lnb_ref[...]
    o_ref[...] = (_silu(y) * _silu(z_ref[...].astype(F32))).astype(o_ref.dtype)


def _conv_branch(proj, dw, bias, lng, lnb, seq_len):
    t = proj.shape[0]
    tm = min(CONV_ROWS, seq_len)
    tiles_per_seq = seq_len // tm
    hb = tm // HALO
    n_hb = t // HALO
    ca, cb, cz = OFF_GLU_A // D_CONV, OFF_GLU_B // D_CONV, OFF_ZCONV // D_CONV
    prev_map = lambda c: (lambda i: (jnp.maximum(i * hb - 1, 0), c))
    next_map = lambda c: (lambda i: (jnp.minimum((i + 1) * hb, n_hb - 1), c))
    row = lambda i: (0, 0)
    return pl.pallas_call(
        functools.partial(_conv_kernel, tiles_per_seq=tiles_per_seq),
        grid=(t // tm,),
        in_specs=[
            pl.BlockSpec((tm, D_CONV), lambda i: (i, ca)),
            pl.BlockSpec((tm, D_CONV), lambda i: (i, cb)),
            pl.BlockSpec((tm, D_CONV), lambda i: (i, cz)),
            pl.BlockSpec((HALO, D_CONV), prev_map(ca)),
            pl.BlockSpec((HALO, D_CONV), prev_map(cb)),
            pl.BlockSpec((HALO, D_CONV), next_map(ca)),
            pl.BlockSpec((HALO, D_CONV), next_map(cb)),
            pl.BlockSpec((CONV_K, D_CONV), row),
            pl.BlockSpec((1, D_CONV), row),
            pl.BlockSpec((1, D_CONV), row),
            pl.BlockSpec((1, D_CONV), row),
        ],
        out_specs=pl.BlockSpec((tm, D_CONV), lambda i: (i, 0)),
        out_shape=jax.ShapeDtypeStruct((t, D_CONV), BF16),
        scratch_shapes=[pltpu.VMEM((tm + 2 * HALO, D_CONV), F32),
                        pltpu.VMEM((SUBLANES, tm + SHIFT_SPAN, D_CONV), F32),
                        pltpu.VMEM((tm, D_CONV), F32)],
        compiler_params=pltpu.CompilerParams(
            dimension_semantics=("parallel",), vmem_limit_bytes=VMEM_LIMIT),
        name="conv_branch",
    )(proj, proj, proj, proj, proj, proj, proj, dw, bias, lng, lnb)


def _merge_kernel(u_ref, a_ref, z_ref, gc_ref, ga_ref, x_ref, wc_ref, wa_ref, wo_ref, pg_ref, o_ref):
    y_conv = jnp.dot(u_ref[...], wc_ref[...], preferred_element_type=F32)
    ag = (a_ref[...].astype(F32) * _silu(z_ref[...].astype(F32))).astype(BF16)
    y_attn = jnp.dot(ag, wa_ref[...], preferred_element_type=F32)
    m = _sigmoid(gc_ref[...].astype(F32)) * y_conv + _sigmoid(ga_ref[...].astype(F32)) * y_attn
    out = jnp.dot(m.astype(BF16), wo_ref[...], preferred_element_type=F32)
    ms = jnp.mean(out * out, axis=-1, keepdims=True)
    o_ref[...] = x_ref[...] + out * lax.rsqrt(ms + EPS) * pg_ref[...]


def _merge_out(u, a, proj, x, wc, wa, wo, pg):
    t = x.shape[0]
    tm = min(MERGE_ROWS, t)
    const = lambda i: (0, 0)
    resident = functools.partial(pl.BlockSpec, index_map=const, pipeline_mode=pl.Buffered(1))
    return pl.pallas_call(
        _merge_kernel,
        grid=(t // tm,),
        in_specs=[
            pl.BlockSpec((tm, D_CONV), lambda i: (i, 0)),
            pl.BlockSpec((tm, D_ATTN), lambda i: (i, 0)),
            pl.BlockSpec((tm, D_ATTN), lambda i: (i, OFF_ZATTN // D_ATTN)),
            pl.BlockSpec((tm, D_MODEL), lambda i: (i, OFF_GCONV // D_MODEL)),
            pl.BlockSpec((tm, D_MODEL), lambda i: (i, OFF_GATTN // D_MODEL)),
            pl.BlockSpec((tm, D_MODEL), lambda i: (i, 0)),
            resident((D_CONV, D_MODEL)),
            resident((D_ATTN, D_MODEL)),
            resident((D_MODEL, D_MODEL)),
            pl.BlockSpec((1, D_MODEL), const),
        ],
        out_specs=pl.BlockSpec((tm, D_MODEL), lambda i: (i, 0)),
        out_shape=jax.ShapeDtypeStruct((t, D_MODEL), F32),
        compiler_params=pltpu.CompilerParams(
            dimension_semantics=("parallel",), vmem_limit_bytes=VMEM_LIMIT),
        name="merge_out",
    )(u, a, proj, proj, proj, x, wc, wa, wo, pg)


def _pair_major(w):
    lead = w.shape[:-1]
    n = w.shape[-1] // HEAD_DIM
    w = w.reshape(lead + (n, 2, 2, ROPE_FREQS))
    w = jnp.swapaxes(w, -2, -3)
    return w.reshape(lead + (n * HEAD_DIM,))


def _pack_w_in(w_in):
    o = [0, D_CONV, 2 * D_CONV, 3 * D_CONV]
    o.append(o[-1] + D_ATTN)
    o.append(o[-1] + D_KV)
    o.append(o[-1] + D_KV)
    o.append(o[-1] + D_ATTN)
    o.append(o[-1] + D_MODEL)
    o.append(o[-1] + D_MODEL)
    w_in = w_in.astype(BF16)
    glu_a, glu_b, z_conv, q, k, v, z_attn, g_conv, g_attn = (
        w_in[..., o[n]:o[n + 1]] for n in range(9))
    return jnp.concatenate(
        [_pair_major(q), z_attn, g_conv, g_attn, glu_a, glu_b, z_conv, _pair_major(k), v], axis=-1)


def _rope_tables(seq_len):
    rows = seq_len // GRID_W
    row_idx = jnp.broadcast_to(jnp.arange(rows, dtype=F32)[:, None], (rows, GRID_W)).reshape(-1)
    col_idx = jnp.broadcast_to(jnp.arange(GRID_W, dtype=F32)[None, :], (rows, GRID_W)).reshape(-1)
    inv_freq = ROPE_THETA ** (-(jnp.arange(ROPE_FREQS, dtype=F32) * 2.0) / ROPE_AXIS_DIM)
    ar = row_idx[:, None] * inv_freq
    ac = col_idx[:, None] * inv_freq
    cos = jnp.concatenate([jnp.cos(ar), jnp.cos(ac), jnp.cos(ar), jnp.cos(ac)], axis=-1)
    sin = jnp.concatenate([-jnp.sin(ar), -jnp.sin(ac), jnp.sin(ar), jnp.sin(ac)], axis=-1)
    return cos, sin


def _layer(x, n_seq, seq_len, tables, p):
    cos, sin = tables
    proj = _inproj(x, p["pre_g"], p["w_in"])
    q, k, vt = _qkv_prep(proj, cos, sin, p["gq"], p["gk"], seq_len)
    score_bound = HEAD_DIM * Q_SCALE * jnp.max(jnp.abs(p["gq"])) * jnp.max(jnp.abs(p["gk"]))
    a = lax.cond(
        score_bound <= MAX_UNSHIFTED,
        functools.partial(_attention, n_seq=n_seq, seq_len=seq_len, body=_attention_unshifted_kernel),
        functools.partial(_attention, n_seq=n_seq, seq_len=seq_len, body=_attention_kernel),
        q, k, vt)
    u = _conv_branch(proj, p["dw"], p["dw_b"], p["ln_g"], p["ln_b"], seq_len)
    return _merge_out(u, a, proj, x, p["w_conv_out"], p["w_attn_o"], p["w_out"], p["post_g"])


def kernel(x_prompt, x_sample, pre_norm_g, w_in, dw_kernel, dw_bias, conv_ln_g, conv_ln_b,
           w_conv_out, q_norm_g, k_norm_g, w_attn_o, w_out, post_norm_g):
    depth = w_in.shape[0]
    w_in_p = _pack_w_in(w_in)
    wc = w_conv_out.astype(BF16)
    wa = w_attn_o.astype(BF16)
    wo = w_out.astype(BF16)
    gq = _pair_major(q_norm_g)
    gk = _pair_major(k_norm_g)

    streams = []
    for x in (x_prompt, x_sample):
        b, s, d = x.shape
        streams.append((x.reshape(b * s, d), b, s, _rope_tables(s)))

    outs = []
    for x2, b, s, tables in streams:
        y = x2
        for l in range(depth):
            p = dict(
                pre_g=pre_norm_g[l][None, :], w_in=w_in_p[l], dw=dw_kernel[l, :, 0, :],
                dw_b=dw_bias[l][None, :], ln_g=conv_ln_g[l][None, :], ln_b=conv_ln_b[l][None, :],
                w_conv_out=wc[l], gq=gq[l][None, :], gk=gk[l][None, :], w_attn_o=wa[l],
                w_out=wo[l], post_g=post_norm_g[l][None, :])
            y = _layer(y, b, s, tables, p)
        outs.append(y.reshape(b, s, d))
    return tuple(outs)
```

```python
import functools
import math

import jax
import jax.numpy as jnp
from jax import lax
from jax.experimental import pallas as pl
from jax.experimental.pallas import tpu as pltpu

F32 = jnp.float32
BF16 = jnp.bfloat16

D_MODEL = 2048
GRID_W = 64
D_CONV = D_MODEL // 2
CONV_K = 31
CONV_PAD = CONV_K // 2
HEAD_DIM = 128
N_HEADS = D_MODEL // HEAD_DIM
N_KV = 4
GROUP = N_HEADS // N_KV
D_ATTN = N_HEADS * HEAD_DIM
D_KV = N_KV * HEAD_DIM
ROPE_AXIS_DIM = HEAD_DIM // 2
ROPE_FREQS = ROPE_AXIS_DIM // 2
ROPE_THETA = 10000.0
EPS = 1e-6
N_IN = 3 * D_CONV + 2 * D_ATTN + 2 * D_KV + 2 * D_MODEL

OFF_Q = 0
OFF_ZATTN = OFF_Q + D_ATTN
OFF_GCONV = OFF_ZATTN + D_ATTN
OFF_GATTN = OFF_GCONV + D_MODEL
OFF_GLU_A = OFF_GATTN + D_MODEL
OFF_GLU_B = OFF_GLU_A + D_CONV
OFF_ZCONV = OFF_GLU_B + D_CONV
OFF_K = OFF_ZCONV + D_CONV
OFF_V = OFF_K + D_KV

HALO = 16
LANES = 128
SUBLANES = 8
CONV_ACC_ROWS = 128
SHIFT_SPAN = (HALO - CONV_PAD + CONV_K - 1) // SUBLANES * SUBLANES
INPROJ_DOT_COLS = 1024
INPROJ_ROWS = 1024
PREP_ROWS = 1024
ATTN_Q_TILE = 256
ATTN_KEY_BLOCK = 8192
ATTN_Q_ROWS_PER_STEP = 4096
ONLINE_KEY_BLOCK = 512
CONV_ROWS = 512
MERGE_ROWS = 256
VMEM_LIMIT = 56 * 1024 * 1024
NEG_BIG = -1e30
MAX_UNSHIFTED = 60.0
NT_DIMS = (((1,), (1,)), ((), ()))
Q_SCALE = (1.0 / math.sqrt(HEAD_DIM)) * math.log2(math.e)


def _sigmoid(x):
    return 1.0 / (1.0 + jnp.exp(-x))


def _silu(x):
    return x * _sigmoid(x)


def _inproj_kernel(x_ref, g_ref, w_ref, o_ref, h_ref):
    @pl.when(pl.program_id(1) == 0)
    def _():
        x = x_ref[...]
        ms = jnp.mean(x * x, axis=-1, keepdims=True)
        h_ref[...] = (x * lax.rsqrt(ms + EPS) * g_ref[...]).astype(BF16)

    for c in range(0, o_ref.shape[1], INPROJ_DOT_COLS):
        cs = slice(c, c + INPROJ_DOT_COLS)
        o_ref[:, cs] = jnp.dot(h_ref[...], w_ref[:, cs], preferred_element_type=F32).astype(o_ref.dtype)


def _inproj(x, g, w):
    t = x.shape[0]
    tm = min(INPROJ_ROWS, t)
    tn = 2 * INPROJ_DOT_COLS
    return pl.pallas_call(
        _inproj_kernel,
        grid=(t // tm, N_IN // tn),
        in_specs=[
            pl.BlockSpec((tm, D_MODEL), lambda i, j: (i, 0)),
            pl.BlockSpec((1, D_MODEL), lambda i, j: (0, 0)),
            pl.BlockSpec((D_MODEL, tn), lambda i, j: (0, j)),
        ],
        out_specs=pl.BlockSpec((tm, tn), lambda i, j: (i, j)),
        out_shape=jax.ShapeDtypeStruct((t, N_IN), BF16),
        scratch_shapes=[pltpu.VMEM((tm, D_MODEL), BF16)],
        compiler_params=pltpu.CompilerParams(
            dimension_semantics=("parallel", "arbitrary"), vmem_limit_bytes=VMEM_LIMIT),
        name="inproj",
    )(x, g, w)


def _gained_tables(g, cos, sin):
    g_rolled = pltpu.roll(jnp.broadcast_to(g, (SUBLANES, HEAD_DIM)), HEAD_DIM // 2, axis=1)[0:1, :]
    return cos * g, sin * g_rolled


def _norm_rope(x, cos_g, sin_g):
    mean_mat = jnp.full((HEAD_DIM, HEAD_DIM), 1.0 / HEAD_DIM, F32)
    ms = jnp.dot(x * x, mean_mat, preferred_element_type=F32)
    return (x * cos_g + pltpu.roll(x, HEAD_DIM // 2, axis=1) * sin_g) * lax.rsqrt(ms + EPS)


def _qkv_prep_kernel(q_ref, k_ref, v_ref, cos_ref, sin_ref, gq_ref, gk_ref,
                     qo_ref, ko_ref, vt_ref):
    cos = cos_ref[...]
    sin = sin_ref[...]
    cos_q, sin_q = _gained_tables(gq_ref[...] * Q_SCALE, cos, sin)
    cos_k, sin_k = _gained_tables(gk_ref[...], cos, sin)
    for h in range(N_HEADS):
        hs = slice(h * HEAD_DIM, (h + 1) * HEAD_DIM)
        y = _norm_rope(q_ref[:, hs].astype(F32), cos_q, sin_q)
        qo_ref[:, hs] = y.astype(BF16)
    for h in range(N_KV):
        hs = slice(h * HEAD_DIM, (h + 1) * HEAD_DIM)
        y = _norm_rope(k_ref[:, hs].astype(F32), cos_k, sin_k)
        ko_ref[:, hs] = y.astype(BF16)
        vt_ref[hs, :] = v_ref[:, hs].astype(F32).T.astype(BF16)


def _qkv_prep(proj, cos, sin, gq, gk, seq_len):
    t = proj.shape[0]
    tm = min(PREP_ROWS, seq_len)
    tiles_per_seq = seq_len // tm
    return pl.pallas_call(
        _qkv_prep_kernel,
        grid=(t // tm,),
        in_specs=[
            pl.BlockSpec((tm, D_ATTN), lambda i: (i, OFF_Q // D_ATTN)),
            pl.BlockSpec((tm, D_KV), lambda i: (i, OFF_K // D_KV)),
            pl.BlockSpec((tm, D_KV), lambda i: (i, OFF_V // D_KV)),
            pl.BlockSpec((tm, HEAD_DIM), lambda i: (i % tiles_per_seq, 0)),
            pl.BlockSpec((tm, HEAD_DIM), lambda i: (i % tiles_per_seq, 0)),
            pl.BlockSpec((1, HEAD_DIM), lambda i: (0, 0)),
            pl.BlockSpec((1, HEAD_DIM), lambda i: (0, 0)),
        ],
        out_specs=[
            pl.BlockSpec((tm, D_ATTN), lambda i: (i, 0)),
            pl.BlockSpec((tm, D_KV), lambda i: (i, 0)),
            pl.BlockSpec((D_KV, tm), lambda i: (0, i)),
        ],
        out_shape=[
            jax.ShapeDtypeStruct((t, D_ATTN), BF16),
            jax.ShapeDtypeStruct((t, D_KV), BF16),
            jax.ShapeDtypeStruct((D_KV, t), BF16),
        ],
        compiler_params=pltpu.CompilerParams(
            dimension_semantics=("parallel",), vmem_limit_bytes=VMEM_LIMIT),
        name="qkv_prep",
    )(proj, proj, proj, cos, sin, gq, gk)


def _attention_kernel(q_ref, k_ref, vt_ref, o_ref, acc_ref, *, tk, n_kv_blocks):
    tq = q_ref.shape[0]
    acc_ref[...] = jnp.zeros_like(acc_ref)

    def body(j, carry):
        off = pl.multiple_of(j * tk, tk)
        kb = k_ref[pl.ds(off, tk), :]
        vb = vt_ref[:, pl.ds(off, tk)]
        new = []
        for g in range(GROUP):
            m, l = carry[g]
            hs = slice(g * HEAD_DIM, (g + 1) * HEAD_DIM)
            s = lax.dot_general(kb, q_ref[:, hs], NT_DIMS, preferred_element_type=F32)
            m_new = jnp.maximum(m, jnp.max(s, axis=0, keepdims=True))
            alpha = jnp.exp2(m - m_new)
            p = jnp.exp2(s - m_new)
            l_new = alpha * l + jnp.sum(p, axis=0, keepdims=True)
            pv = jnp.dot(vb, p.astype(BF16), preferred_element_type=F32)
            acc_ref[hs, :] = alpha * acc_ref[hs, :] + pv
            new.append((m_new, l_new))
        return tuple(new)

    init = tuple((jnp.full((1, tq), NEG_BIG, F32), jnp.zeros((1, tq), F32)) for _ in range(GROUP))
    stats = lax.fori_loop(0, n_kv_blocks, body, init)
    for g in range(GROUP):
        hs = slice(g * HEAD_DIM, (g + 1) * HEAD_DIM)
        out = acc_ref[hs, :] / stats[g][1]
        o_ref[:, hs] = out.T.astype(o_ref.dtype)


def _attention_unshifted_kernel(q_ref, k_ref, vt_ref, o_ref, acc_ref, s0_ref, *, tq, tk, n_kv_blocks):
    heads = [slice(g * HEAD_DIM, (g + 1) * HEAD_DIM) for g in range(GROUP)]
    n_trips = (q_ref.shape[0] // tq) * n_kv_blocks
    acc_ref[...] = jnp.zeros_like(acc_ref)

    def offsets(t):
        return pl.multiple_of((t // n_kv_blocks) * tq, tq), pl.multiple_of((t % n_kv_blocks) * tk, tk)

    def scores(t, g):
        qoff, koff = offsets(t)
        return lax.dot_general(k_ref[pl.ds(koff, tk), :], q_ref[pl.ds(qoff, tq), heads[g]], NT_DIMS,
                               preferred_element_type=F32)

    s0_ref[...] = scores(0, 0)

    def body(t, carry):
        qoff, koff = offsets(t)
        vb = vt_ref[:, pl.ds(koff, tk)]
        new = []
        s = s0_ref[...]
        for g in range(GROUP):
            if g + 1 < GROUP:
                s_next = scores(t, g + 1)
            else:
                s_next = scores(jnp.minimum(t + 1, n_trips - 1), 0)
            p = jnp.exp2(s)
            new.append(carry[g] + jnp.sum(p, axis=0, keepdims=True))
            acc_ref[heads[g], :] += jnp.dot(vb, p.astype(BF16), preferred_element_type=F32)
            s = s_next
        s0_ref[...] = s
        tile_done = (t % n_kv_blocks) == n_kv_blocks - 1

        @pl.when(tile_done)
        def _():
            for g in range(GROUP):
                out = acc_ref[heads[g], :] / new[g]
                o_ref[pl.ds(qoff, tq), heads[g]] = out.T.astype(o_ref.dtype)
            acc_ref[...] = jnp.zeros_like(acc_ref)

        return tuple(jnp.where(tile_done, 0.0, l) for l in new)

    init = tuple(jnp.zeros((1, tq), F32) for _ in range(GROUP))
    lax.fori_loop(0, n_trips, body, init)


def _attention(q, k, vt, n_seq, seq_len, body):
    t = k.shape[0]
    unshifted = body is _attention_unshifted_kernel
    tq = min(ATTN_Q_TILE, seq_len)
    gw = GROUP * HEAD_DIM
    scratch = [pltpu.VMEM((gw, tq), F32)]
    if unshifted:
        tk = min(ATTN_KEY_BLOCK, seq_len)
        bq = min(ATTN_Q_ROWS_PER_STEP, seq_len)
        scratch.append(pltpu.VMEM((tk, tq), F32))
        kern = functools.partial(body, tq=tq, tk=tk, n_kv_blocks=seq_len // tk)
    else:
        tk = min(ONLINE_KEY_BLOCK, seq_len)
        bq = tq
        kern = functools.partial(body, tk=tk, n_kv_blocks=seq_len // tk)
    q_blocks = seq_len // bq
    return pl.pallas_call(
        kern,
        grid=(n_seq, N_KV, q_blocks),
        in_specs=[
            pl.BlockSpec((bq, gw), lambda b, h, i: (b * q_blocks + i, h)),
            pl.BlockSpec((seq_len, HEAD_DIM), lambda b, h, i: (b, h)),
            pl.BlockSpec((HEAD_DIM, seq_len), lambda b, h, i: (h, b)),
        ],
        out_specs=pl.BlockSpec((bq, gw), lambda b, h, i: (b * q_blocks + i, h)),
        out_shape=jax.ShapeDtypeStruct((t, D_ATTN), BF16),
        scratch_shapes=scratch,
        compiler_params=pltpu.CompilerParams(
            dimension_semantics=("parallel", "parallel", "arbitrary"), vmem_limit_bytes=VMEM_LIMIT),
        name="attention_unshifted" if unshifted else "attention_online",
    )(q, k, vt)


def _conv_kernel(a_ref, b_ref, z_ref, ap_ref, bp_ref, an_ref, bn_ref,
                 w_ref, bias_ref, lng_ref, lnb_ref, o_ref, ext_ref, sh_ref, cv_ref, *, tiles_per_seq):
    tm = a_ref.shape[0]
    pos = pl.program_id(0) % tiles_per_seq

    def glu(a, b):
        return a.astype(F32) * _sigmoid(b.astype(F32))

    ext_ref[0:HALO, :] = jnp.where(pos != 0, glu(ap_ref[...], bp_ref[...]), 0.0)
    ext_ref[HALO:HALO + tm, :] = glu(a_ref[...], b_ref[...])
    ext_ref[HALO + tm:2 * HALO + tm, :] = jnp.where(
        pos != tiles_per_seq - 1, glu(an_ref[...], bn_ref[...]), 0.0)

    span = tm + SHIFT_SPAN
    sh_ref[0] = ext_ref[0:span, :]
    for s in range(1, SUBLANES):
        sh_ref[s] = ext_ref[s:s + span, :]

    base = HALO - CONV_PAD
    rows = min(tm, CONV_ACC_ROWS)
    for r0 in range(0, tm, rows):
        for c in range(D_CONV // LANES):
            cs = slice(c * LANES, (c + 1) * LANES)
            acc = jnp.zeros((rows, LANES), F32)
            for k in range(CONV_K):
                s, a = (base + k) % SUBLANES, (base + k) // SUBLANES * SUBLANES
                acc = acc + sh_ref[s, r0 + a:r0 + a + rows, cs] * w_ref[k:k + 1, cs]
            cv_ref[r0:r0 + rows, cs] = acc + bias_ref[:, cs]

    u = cv_ref[...]
    mu = jnp.mean(u, axis=-1, keepdims=True)
    uc = u - mu
    var = jnp.mean(uc * uc, axis=-1, keepdims=True)
    y = uc * lax.rsqrt(var + EPS) * lng_ref[...] + lnb_ref[...]
    o_ref[...] = (_silu(y) * _silu(z_ref[...].astype(F32))).astype(o_ref.dtype)


def _conv_branch(proj, dw, bias, lng, lnb, seq_len):
    t = proj.shape[0]
    tm = min(CONV_ROWS, seq_len)
    tiles_per_seq = seq_len // tm
    hb = tm // HALO
    n_hb = t // HALO
    ca, cb, cz = OFF_GLU_A // D_CONV, OFF_GLU_B // D_CONV, OFF_ZCONV // D_CONV
    prev_map = lambda c: (lambda i: (jnp.maximum(i * hb - 1, 0), c))
    next_map = lambda c: (lambda i: (jnp.minimum((i + 1) * hb, n_hb - 1), c))
    row = lambda i: (0, 0)
    return pl.pallas_call(
        functools.partial(_conv_kernel, tiles_per_seq=tiles_per_seq),
        grid=(t // tm,),
        in_specs=[
            pl.BlockSpec((tm, D_CONV), lambda i: (i, ca)),
            pl.BlockSpec((tm, D_CONV), lambda i: (i, cb)),
            pl.BlockSpec((tm, D_CONV), lambda i: (i, cz)),
            pl.BlockSpec((HALO, D_CONV), prev_map(ca)),
            pl.BlockSpec((HALO, D_CONV), prev_map(cb)),
            pl.BlockSpec((HALO, D_CONV), next_map(ca)),
            pl.BlockSpec((HALO, D_CONV), next_map(cb)),
            pl.BlockSpec((CONV_K, D_CONV), row),
            pl.BlockSpec((1, D_CONV), row),
            pl.BlockSpec((1, D_CONV), row),
            pl.BlockSpec((1, D_CONV), row),
        ],
        out_specs=pl.BlockSpec((tm, D_CONV), lambda i: (i, 0)),
        out_shape=jax.ShapeDtypeStruct((t, D_CONV), BF16),
        scratch_shapes=[pltpu.VMEM((tm + 2 * HALO, D_CONV), F32),
                        pltpu.VMEM((SUBLANES, tm + SHIFT_SPAN, D_CONV), F32),
                        pltpu.VMEM((tm, D_CONV), F32)],
        compiler_params=pltpu.CompilerParams(
            dimension_semantics=("parallel",), vmem_limit_bytes=VMEM_LIMIT),
        name="conv_branch",
    )(proj, proj, proj, proj, proj, proj, proj, dw, bias, lng, lnb)


def _merge_kernel(u_ref, a_ref, z_ref, gc_ref, ga_ref, x_ref, wc_ref, wa_ref, wo_ref, pg_ref, o_ref):
    y_conv = jnp.dot(u_ref[...], wc_ref[...], preferred_element_type=F32)
    ag = (a_ref[...].astype(F32) * _silu(z_ref[...].astype(F32))).astype(BF16)
    y_attn = jnp.dot(ag, wa_ref[...], preferred_element_type=F32)
    m = _sigmoid(gc_ref[...].astype(F32)) * y_conv + _sigmoid(ga_ref[...].astype(F32)) * y_attn
    out = jnp.dot(m.astype(BF16), wo_ref[...], preferred_element_type=F32)
    ms = jnp.mean(out * out, axis=-1, keepdims=True)
    o_ref[...] = x_ref[...] + out * lax.rsqrt(ms + EPS) * pg_ref[...]


def _merge_out(u, a, proj, x, wc, wa, wo, pg):
    t = x.shape[0]
    tm = min(MERGE_ROWS, t)
    const = lambda i: (0, 0)
    resident = functools.partial(pl.BlockSpec, index_map=const, pipeline_mode=pl.Buffered(1))
    return pl.pallas_call(
        _merge_kernel,
        grid=(t // tm,),
        in_specs=[
            pl.BlockSpec((tm, D_CONV), lambda i: (i, 0)),
            pl.BlockSpec((tm, D_ATTN), lambda i: (i, 0)),
            pl.BlockSpec((tm, D_ATTN), lambda i: (i, OFF_ZATTN // D_ATTN)),
            pl.BlockSpec((tm, D_MODEL), lambda i: (i, OFF_GCONV // D_MODEL)),
            pl.BlockSpec((tm, D_MODEL), lambda i: (i, OFF_GATTN // D_MODEL)),
            pl.BlockSpec((tm, D_MODEL), lambda i: (i, 0)),
            resident((D_CONV, D_MODEL)),
            resident((D_ATTN, D_MODEL)),
            resident((D_MODEL, D_MODEL)),
            pl.BlockSpec((1, D_MODEL), const),
        ],
        out_specs=pl.BlockSpec((tm, D_MODEL), lambda i: (i, 0)),
        out_shape=jax.ShapeDtypeStruct((t, D_MODEL), F32),
        compiler_params=pltpu.CompilerParams(
            dimension_semantics=("parallel",), vmem_limit_bytes=VMEM_LIMIT),
        name="merge_out",
    )(u, a, proj, proj, proj, x, wc, wa, wo, pg)


def _pair_major(w):
    lead = w.shape[:-1]
    n = w.shape[-1] // HEAD_DIM
    w = w.reshape(lead + (n, 2, 2, ROPE_FREQS))
    w = jnp.swapaxes(w, -2, -3)
    return w.reshape(lead + (n * HEAD_DIM,))


def _pack_w_in(w_in):
    o = [0, D_CONV, 2 * D_CONV, 3 * D_CONV]
    o.append(o[-1] + D_ATTN)
    o.append(o[-1] + D_KV)
    o.append(o[-1] + D_KV)
    o.append(o[-1] + D_ATTN)
    o.append(o[-1] + D_MODEL)
    o.append(o[-1] + D_MODEL)
    w_in = w_in.astype(BF16)
    glu_a, glu_b, z_conv, q, k, v, z_attn, g_conv, g_attn = (
        w_in[..., o[n]:o[n + 1]] for n in range(9))
    return jnp.concatenate(
        [_pair_major(q), z_attn, g_conv, g_attn, glu_a, glu_b, z_conv, _pair_major(k), v], axis=-1)


def _rope_tables(seq_len):
    rows = seq_len // GRID_W
    row_idx = jnp.broadcast_to(jnp.arange(rows, dtype=F32)[:, None], (rows, GRID_W)).reshape(-1)
    col_idx = jnp.broadcast_to(jnp.arange(GRID_W, dtype=F32)[None, :], (rows, GRID_W)).reshape(-1)
    inv_freq = ROPE_THETA ** (-(jnp.arange(ROPE_FREQS, dtype=F32) * 2.0) / ROPE_AXIS_DIM)
    ar = row_idx[:, None] * inv_freq
    ac = col_idx[:, None] * inv_freq
    cos = jnp.concatenate([jnp.cos(ar), jnp.cos(ac), jnp.cos(ar), jnp.cos(ac)], axis=-1)
    sin = jnp.concatenate([-jnp.sin(ar), -jnp.sin(ac), jnp.sin(ar), jnp.sin(ac)], axis=-1)
    return cos, sin


def _layer(x, n_seq, seq_len, tables, p):
    cos, sin = tables
    proj = _inproj(x, p["pre_g"], p["w_in"])
    q, k, vt = _qkv_prep(proj, cos, sin, p["gq"], p["gk"], seq_len)
    score_bound = HEAD_DIM * Q_SCALE * jnp.max(jnp.abs(p["gq"])) * jnp.max(jnp.abs(p["gk"]))
    a = lax.cond(
        score_bound <= MAX_UNSHIFTED,
        functools.partial(_attention, n_seq=n_seq, seq_len=seq_len, body=_attention_unshifted_kernel),
        functools.partial(_attention, n_seq=n_seq, seq_len=seq_len, body=_attention_kernel),
        q, k, vt)
    u = _conv_branch(proj, p["dw"], p["dw_b"], p["ln_g"], p["ln_b"], seq_len)
    return _merge_out(u, a, proj, x, p["w_conv_out"], p["w_attn_o"], p["w_out"], p["post_g"])


def kernel(x_prompt, x_sample, pre_norm_g, w_in, dw_kernel, dw_bias, conv_ln_g, conv_ln_b,
           w_conv_out, q_norm_g, k_norm_g, w_attn_o, w_out, post_norm_g):
    depth = w_in.shape[0]
    w_in_p = _pack_w_in(w_in)
    wc = w_conv_out.astype(BF16)
    wa = w_attn_o.astype(BF16)
    wo = w_out.astype(BF16)
    gq = _pair_major(q_norm_g)
    gk = _pair_major(k_norm_g)

    streams = []
    for x in (x_prompt, x_sample):
        b, s, d = x.shape
        streams.append((x.reshape(b * s, d), b, s, _rope_tables(s)))

    outs = []
    for x2, b, s, tables in streams:
        y = x2
        for l in range(depth):
            p = dict(
                pre_g=pre_norm_g[l][None, :], w_in=w_in_p[l], dw=dw_kernel[l, :, 0, :],
                dw_b=dw_bias[l][None, :], ln_g=conv_ln_g[l][None, :], ln_b=conv_ln_b[l][None, :],
                w_conv_out=wc[l], gq=gq[l][None, :], gk=gk[l][None, :], w_attn_o=wa[l],
                w_out=wo[l], post_g=post_norm_g[l][None, :])
            y = _layer(y, b, s, tables, p)
        outs.append(y.reshape(b, s, d))
    return tuple(outs)
```
